```python
import math
import jax
import jax.numpy as jnp
from jax import lax
import numpy as np

D_MODEL = 1024
BATCH = 4
SEQ = 4096
DEPTH = 2
DEC_BATCH = 128
DEC_SEQ = 4
PAST_LEN = 16384
PAGE_SIZE = 128

GDN_HEADS = 4
GDN_DK = 128
GDN_DV = 128
GDN_CONV = 4
GDN_CHUNK = 64
GDN_QKV = GDN_HEADS * (2 * GDN_DK + GDN_DV)
MLA_HEADS = 4
MLA_Q_RANK = 256
MLA_KV_RANK = 128
MLA_NOPE = 64
MLA_ROPE = 32
MLA_V = 64
ROPE_BASE = 10000.0
MLA_SCALE = (MLA_NOPE + MLA_ROPE) ** -0.5
FOX_HEADS = 4
FOX_HD = 64
FOX_SCALE = FOX_HD ** -0.5
Q_BLOCK = 128
MIX_WIDTH = GDN_HEADS * GDN_DV + MLA_HEADS * MLA_V + FOX_HEADS * FOX_HD
IN_SIZES = (GDN_QKV, GDN_HEADS * GDN_DV, GDN_HEADS, GDN_HEADS,
            MLA_Q_RANK, MLA_KV_RANK, MLA_ROPE,
            FOX_HEADS * FOX_HD, FOX_HEADS * FOX_HD, FOX_HEADS * FOX_HD, FOX_HEADS)
N_IN = sum(IN_SIZES)
D_FF = 2816
N_EXPERTS = 8
TOP_K = 2
D_FF_EXPERT = 2816
N_DENSE = (DEPTH + 1) // 2
N_MOE = DEPTH // 2
EPS = 1e-6
NEG = -1e30

kernel_name = 'hybrid_gdn_mla_fox_adaln_step'


def _split_points():
    pts, acc = [], 0
    for s in IN_SIZES[:-1]:
        acc += s
        pts.append(acc)
    return pts


def rmsnorm(x, g):
    xf = x.astype(jnp.float32)
    y = xf * lax.rsqrt(jnp.mean(xf * xf, axis=-1, keepdims=True) + EPS)
    return (y * g.astype(jnp.float32)).astype(x.dtype)


def l2norm(x):
    xf = x.astype(jnp.float32)
    return xf * lax.rsqrt(jnp.sum(xf * xf, axis=-1, keepdims=True) + EPS)


def rope_cos_sin(pos):
    half = MLA_ROPE // 2
    inv = jnp.exp(-math.log(ROPE_BASE) * jnp.arange(half, dtype=jnp.float32) / half)
    ang = pos.astype(jnp.float32)[:, None] * inv[None, :]
    return jnp.cos(ang), jnp.sin(ang)


def apply_rope(x, cos, sin):
    half = MLA_ROPE // 2
    xf = x.astype(jnp.float32)
    x1, x2 = xf[..., :half], xf[..., half:]
    return jnp.concatenate([x1 * cos - x2 * sin, x2 * cos + x1 * sin], axis=-1).astype(x.dtype)


def causal_dwconv(x, buf, w):
    xp = jnp.concatenate([buf.astype(x.dtype), x], axis=1)
    y = lax.conv_general_dilated(xp, w.astype(x.dtype)[:, None, :], window_strides=(1,), padding='VALID',
                                 dimension_numbers=('NWC', 'WIO', 'NWC'), feature_group_count=x.shape[-1])
    return jax.nn.silu(y), xp[:, xp.shape[1] - (GDN_CONV - 1):]


def gated_delta_chunked(q, k, v, g, beta, S0):
    B, T, H, DK = q.shape
    C = min(GDN_CHUNK, T)
    pad = (-T) % C
    q = q * (DK ** -0.5)
    if pad:
        padt = lambda a: jnp.pad(a, [(0, 0), (0, pad)] + [(0, 0)] * (a.ndim - 2))
        q, k, v, g, beta = (padt(a) for a in (q, k, v, g, beta))
    N = (T + pad) // C

    def chunks(a):
        a = a.reshape((B, N, C, H) + a.shape[3:])
        return jnp.moveaxis(jnp.moveaxis(a, 1, 0), 3, 2)

    q, k, v, g, beta = (chunks(a) for a in (q, k, v, g, beta))
    gc = jnp.cumsum(g, axis=-1)
    kb = k * beta[..., None]
    vb = v * beta[..., None]
    idx = jnp.arange(C)
    incl = idx[:, None] >= idx[None, :]
    strict = idx[:, None] > idx[None, :]
    diff = gc[..., :, None] - gc[..., None, :]
    decay = jnp.where(incl, jnp.exp(jnp.where(incl, diff, 0.0)), 0.0)
    L = jnp.where(strict, jnp.einsum('nbhcd,nbhsd->nbhcs', kb, k) * decay, 0.0)
    M = L + jnp.eye(C, dtype=L.dtype)
    solve = lambda rhs: lax.linalg.triangular_solve(M, rhs, left_side=True, lower=True, unit_diagonal=True)
    u = solve(vb)
    w = solve(kb * jnp.exp(gc)[..., None])
    a_qk = jnp.einsum('nbhcd,nbhsd->nbhcs', q, k) * decay

    def step(S, xs):
        qc, kc, uc, wc, gcc, aqk = xs
        v_new = uc - jnp.einsum('bhcd,bhde->bhce', wc, S)
        o = (jnp.einsum('bhcd,bhde->bhce', qc * jnp.exp(gcc)[..., None], S)
             + jnp.einsum('bhcs,bhse->bhce', aqk, v_new))
        g_last = gcc[..., -1]
        S = (S * jnp.exp(g_last)[..., None, None]
             + jnp.einsum('bhcd,bhce->bhde', kc * jnp.exp(g_last[..., None] - gcc)[..., None], v_new))
        return S, o

    S, o = lax.scan(step, S0, (q, k, u, w, gc, a_qk))
    o = jnp.swapaxes(jnp.moveaxis(o, 0, 1), 2, 3).reshape(B, N * C, H, -1)[:, :T]
    return o, S


def prompt_attend(q_side, logits_fn, pv_fn):
    B, H, T = q_side[0].shape[:3]
    qb = min(Q_BLOCK, T)
    nb = T // qb
    blk = tuple(jnp.moveaxis(a.reshape(a.shape[:2] + (nb, qb) + a.shape[3:]), 2, 0) for a in q_side)
    kpos = jnp.arange(T)

    def one(args):
        qs, i = args
        qpos = i * qb + jnp.arange(qb)
        s = jnp.where(kpos[None, :] <= qpos[:, None], logits_fn(qs), NEG)
        return pv_fn(jax.nn.softmax(s, axis=-1))

    o = lax.map(one, (blk, jnp.arange(nb)))
    return jnp.moveaxis(o, 0, 2).reshape(B, H, T, -1)


def online_update(carry, s, pv_fn):
    m, l, acc = carry
    m_new = jnp.maximum(m, jnp.max(s, axis=-1))
    alpha = jnp.exp(m - m_new)
    pexp = jnp.exp(s - m_new[..., None])
    return (m_new, l * alpha + jnp.sum(pexp, axis=-1),
            acc * alpha[..., None] + pv_fn(pexp).astype(jnp.float32))


def paged_attend(page_xs, page_fn, s_new, pv_new, acc_dim):
    b, h, t = s_new.shape[:3]
    carry = (jnp.full((b, h, t), NEG, jnp.float32), jnp.zeros((b, h, t), jnp.float32),
             jnp.zeros((b, h, t, acc_dim), jnp.float32))

    def step(c, xp):
        s, pv = page_fn(xp)
        return online_update(c, s, pv), None

    carry, _ = lax.scan(step, carry, page_xs)
    m, l, acc = online_update(carry, s_new, pv_new)
    return acc / l[..., None]


def swiglu(h, wg, wu, wd):
    return (jax.nn.silu(h @ wg) * (h @ wu)) @ wd


def moe_ffn(h, router, wg, wu, wd):
    shp = h.shape
    hf = h.reshape(-1, shp[-1])
    logits = (hf @ router).astype(jnp.float32)
    top_val, top_idx = lax.top_k(logits, TOP_K)
    top_w = jax.nn.softmax(top_val, axis=-1)
    gates = jnp.sum(jax.nn.one_hot(top_idx, N_EXPERTS, dtype=jnp.float32) * top_w[..., None], axis=1)
    out = jnp.zeros(hf.shape, jnp.float32)
    for e in range(N_EXPERTS):
        out = out + gates[:, e:e + 1] * swiglu(hf, wg[e], wu[e], wd[e]).astype(jnp.float32)
    return out.astype(h.dtype).reshape(shp)


def layer_step(x, c, l, p, past):
    B, T, _ = x.shape
    f32 = jnp.float32
    mod = jax.nn.silu(c) @ p['w_ada'][l] + p['b_ada'][l]
    sh1, sc1, gt1, sh2, sc2, gt2 = [m[:, None, :] for m in jnp.split(mod, 6, axis=-1)]
    h = rmsnorm(x, p['g_norm1'][l]) * (1 + sc1) + sh1
    (qkv_a, z_a, b_a, a_a, cq_b, ckv_b, kpe_b, q_c, k_c, v_c, f_c) = jnp.split(
        h @ p['w_in'][l], _split_points(), axis=-1)
    pos = jnp.arange(T) + (0 if past is None else PAST_LEN)
    causal = jnp.arange(T)[None, :] <= jnp.arange(T)[:, None]

    buf = jnp.zeros((B, GDN_CONV - 1, GDN_QKV), x.dtype) if past is None else past['gdn_conv']
    qkv_a, new_conv = causal_dwconv(qkv_a, buf, p['gdn_conv_w'][l])
    q_a, k_a, v_a = jnp.split(qkv_a, [GDN_HEADS * GDN_DK, 2 * GDN_HEADS * GDN_DK], axis=-1)
    q_a = l2norm(q_a.reshape(B, T, GDN_HEADS, GDN_DK))
    k_a = l2norm(k_a.reshape(B, T, GDN_HEADS, GDN_DK))
    v_a = v_a.reshape(B, T, GDN_HEADS, GDN_DV).astype(f32)
    beta = jax.nn.sigmoid(b_a.astype(f32))
    g_a = -jnp.exp(p['gdn_a_log'][l].astype(f32)) * jax.nn.softplus(a_a.astype(f32) + p['gdn_dt_bias'][l].astype(f32))
    S0 = (jnp.zeros((B, GDN_HEADS, GDN_DK, GDN_DV), f32) if past is None
          else past['gdn_state'].astype(f32))
    o_a, S_a = gated_delta_chunked(q_a, k_a, v_a, g_a, beta, S0)
    o_a = rmsnorm(o_a, p['gdn_norm_g'][l]) * jax.nn.silu(z_a.reshape(B, T, GDN_HEADS, GDN_DV).astype(f32))
    o_a = o_a.reshape(B, T, GDN_HEADS * GDN_DV).astype(x.dtype)

    qb_full = (rmsnorm(cq_b, p['mla_q_norm_g'][l]) @ p['mla_w_qb'][l]).reshape(B, T, MLA_HEADS, MLA_NOPE + MLA_ROPE)
    cos, sin = rope_cos_sin(pos)
    q_nope = qb_full[..., :MLA_NOPE]
    q_pe = jnp.swapaxes(apply_rope(qb_full[..., MLA_NOPE:], cos[:, None, :], sin[:, None, :]), 1, 2)
    k_pe = apply_rope(kpe_b, cos, sin)
    lat = rmsnorm(ckv_b, p['mla_kv_norm_g'][l])
    w_kvb = p['mla_w_kvb'][l].reshape(MLA_KV_RANK, MLA_HEADS, MLA_NOPE + MLA_V)
    w_kb, w_vb = w_kvb[..., :MLA_NOPE], w_kvb[..., MLA_NOPE:]
    q_abs = jnp.einsum('bthn,rhn->bhtr', q_nope, w_kb)
    if past is None:
        def logits_b(qs):
            qa_blk, qp_blk = qs
            return (jnp.einsum('bhqr,bkr->bhqk', qa_blk, lat)
                    + jnp.einsum('bhqe,bke->bhqk', qp_blk, k_pe)).astype(f32) * MLA_SCALE
        o_lat = prompt_attend((q_abs, q_pe), logits_b,
                              lambda pr: jnp.einsum('bhqk,bkr->bhqr', pr.astype(lat.dtype), lat))
    else:
        pt = past['page_table']
        pool_lat, pool_pe = past['mla_lat'], past['mla_pe']

        def page_b(pid):
            lat_p = pool_lat[l, pid]
            pe_p = pool_pe[l, pid]
            s = (jnp.einsum('bhtr,bpr->bhtp', q_abs, lat_p)
                 + jnp.einsum('bhte,bpe->bhtp', q_pe, pe_p)).astype(f32) * MLA_SCALE
            return s, lambda pr: jnp.einsum('bhtp,bpr->bhtr', pr.astype(lat_p.dtype), lat_p)
        s_new = (jnp.einsum('bhtr,bsr->bhts', q_abs, lat)
                 + jnp.einsum('bhte,bse->bhts', q_pe, k_pe)).astype(f32) * MLA_SCALE
        s_new = jnp.where(causal, s_new, NEG)
        o_lat = paged_attend(pt.T, page_b, s_new,
                             lambda pr: jnp.einsum('bhts,bsr->bhtr', pr.astype(lat.dtype), lat), MLA_KV_RANK)
    o_b = jnp.einsum('bhtr,rhv->bthv', o_lat.astype(x.dtype), w_vb).reshape(B, T, MLA_HEADS * MLA_V)

    k_c4 = k_c.reshape(B, T, FOX_HEADS, FOX_HD)
    v_c4 = v_c.reshape(B, T, FOX_HEADS, FOX_HD)
    qh = jnp.swapaxes(q_c.reshape(B, T, FOX_HEADS, FOX_HD), 1, 2)
    kh = jnp.swapaxes(k_c4, 1, 2)
    vh = jnp.swapaxes(v_c4, 1, 2)
    logf = jax.nn.log_sigmoid(f_c.astype(f32) + p['fox_f_bias'][l].astype(f32))
    G = jnp.swapaxes(jnp.cumsum(logf, axis=1), 1, 2)
    if past is None:
        def logits_c(qs):
            q_blk, g_blk = qs
            return (jnp.einsum('bhqd,bhkd->bhqk', q_blk, kh).astype(f32) * FOX_SCALE
                    + g_blk[..., None] - G[:, :, None, :])
        o_c = prompt_attend((qh, G), logits_c,
                            lambda pr: jnp.einsum('bhqk,bhkd->bhqd', pr.astype(vh.dtype), vh))
    else:
        pt = past['page_table']
        pool_k, pool_v = past['fox_k'], past['fox_v']
        nbat, npg = pt.shape
        lp = past['fox_logf'][l, pt].astype(f32).reshape(nbat, npg * PAGE_SIZE, FOX_HEADS)
        rc = lax.cumsum(lp, axis=1, reverse=True)
        suffix = jnp.concatenate([rc[:, 1:], jnp.zeros_like(rc[:, :1])], axis=1)
        suffix = jnp.transpose(suffix.reshape(nbat, npg, PAGE_SIZE, FOX_HEADS), (1, 0, 3, 2))

        def page_c(xp):
            pid, suf = xp
            k_p = pool_k[l, pid]
            v_p = pool_v[l, pid]
            s = (jnp.einsum('bhtd,bphd->bhtp', qh, k_p).astype(f32) * FOX_SCALE
                 + G[..., None] + suf[:, :, None, :])
            return s, lambda pr: jnp.einsum('bhtp,bphd->bhtd', pr.astype(v_p.dtype), v_p)
        s_new = (jnp.einsum('bhtd,bhsd->bhts', qh, kh).astype(f32) * FOX_SCALE
                 + G[..., :, None] - G[..., None, :])
        s_new = jnp.where(causal, s_new, NEG)
        o_c = paged_attend((pt.T, suffix), page_c, s_new,
                           lambda pr: jnp.einsum('bhts,bhsd->bhtd', pr.astype(vh.dtype), vh), FOX_HD)
    o_c = jnp.swapaxes(o_c, 1, 2).reshape(B, T, FOX_HEADS * FOX_HD).astype(x.dtype)

    o = jnp.concatenate([o_a, o_b, o_c], axis=-1) @ p['w_out'][l]
    x = x + (1 + gt1) * o

    h2 = rmsnorm(x, p['g_norm2'][l]) * (1 + sc2) + sh2
    i = l // 2
    if l % 2 == 0:
        f = swiglu(h2, p['ffn_w_gate'][i], p['ffn_w_up'][i], p['ffn_w_down'][i])
    else:
        f = moe_ffn(h2, p['moe_router'][i], p['moe_w_gate'][i], p['moe_w_up'][i], p['moe_w_down'][i])
    x = x + (1 + gt2) * f

    new = dict(mla_lat=lat, mla_pe=k_pe, fox_k=k_c4, fox_v=v_c4, fox_logf=logf.astype(x.dtype),
               gdn_state=S_a.astype(x.dtype), gdn_conv=new_conv)
    return x, new


def setup_inputs(seed: int = 0) -> dict:
    key = jax.random.key(seed)
    keys = jax.random.split(key, 48)
    counter = iter(range(48))
    nk = lambda: keys[next(counter)]
    nrm = lambda shape, scale=1.0: scale * jax.random.normal(nk(), shape, jnp.float32)
    gain = lambda shape: 1.0 + nrm(shape, 0.05)
    D = D_MODEL
    n_pages = PAST_LEN // PAGE_SIZE
    n_pool = (DEC_BATCH * n_pages * 5) // 4
    inp = {}
    inp['x_prompt'] = nrm((BATCH, SEQ, D))
    inp['x_sample'] = nrm((DEC_BATCH, DEC_SEQ, D))
    inp['cache_mla_latent'] = nrm((DEPTH, n_pool, PAGE_SIZE, MLA_KV_RANK))
    inp['cache_mla_rope'] = nrm((DEPTH, n_pool, PAGE_SIZE, MLA_ROPE))
    inp['cache_fox_k'] = nrm((DEPTH, n_pool, PAGE_SIZE, FOX_HEADS, FOX_HD))
    inp['cache_fox_v'] = nrm((DEPTH, n_pool, PAGE_SIZE, FOX_HEADS, FOX_HD))
    inp['cache_fox_logf'] = jax.nn.log_sigmoid(3.0 + nrm((DEPTH, n_pool, PAGE_SIZE, FOX_HEADS), 0.5))
    inp['state_gdn'] = nrm((DEPTH, DEC_BATCH, GDN_HEADS, GDN_DK, GDN_DV), 0.5)
    inp['state_gdn_conv'] = nrm((DEPTH, DEC_BATCH, GDN_CONV - 1, GDN_QKV))
    inp['page_table'] = jax.random.permutation(nk(), n_pool)[:DEC_BATCH * n_pages].reshape(
        DEC_BATCH, n_pages).astype(jnp.int32)
    inp['c_prompt'] = nrm((BATCH, D))
    inp['c_sample'] = nrm((DEC_BATCH, D))
    inp['w_ada'] = nrm((DEPTH, D, 6 * D), 0.2 * D ** -0.5)
    inp['b_ada'] = nrm((DEPTH, 6 * D), 0.01)
    inp['g_norm1'] = gain((DEPTH, D))
    inp['g_norm2'] = gain((DEPTH, D))
    inp['w_in'] = nrm((DEPTH, D, N_IN), D ** -0.5)
    inp['gdn_conv_w'] = nrm((DEPTH, GDN_CONV, GDN_QKV), GDN_CONV ** -0.5)
    inp['gdn_a_log'] = jnp.log(jax.random.uniform(nk(), (DEPTH, GDN_HEADS), jnp.float32, 1.0, 16.0))
    dt = jnp.exp(jax.random.uniform(nk(), (DEPTH, GDN_HEADS), jnp.float32, math.log(1e-3), math.log(1e-1)))
    inp['gdn_dt_bias'] = dt + jnp.log(-jnp.expm1(-dt))
    inp['gdn_norm_g'] = gain((DEPTH, GDN_DV))
    inp['mla_q_norm_g'] = gain((DEPTH, MLA_Q_RANK))
    inp['mla_w_qb'] = nrm((DEPTH, MLA_Q_RANK, MLA_HEADS * (MLA_NOPE + MLA_ROPE)), MLA_Q_RANK ** -0.5)
    inp['mla_kv_norm_g'] = gain((DEPTH, MLA_KV_RANK))
    inp['mla_w_kvb'] = nrm((DEPTH, MLA_KV_RANK, MLA_HEADS * (MLA_NOPE + MLA_V)), MLA_KV_RANK ** -0.5)
    inp['fox_f_bias'] = 3.0 + nrm((DEPTH, FOX_HEADS), 0.5)
    inp['w_out'] = nrm((DEPTH, MIX_WIDTH, D), MIX_WIDTH ** -0.5)
    inp['ffn_w_gate'] = nrm((N_DENSE, D, D_FF), D ** -0.5)
    inp['ffn_w_up'] = nrm((N_DENSE, D, D_FF), D ** -0.5)
    inp['ffn_w_down'] = nrm((N_DENSE, D_FF, D), D_FF ** -0.5)
    inp['moe_router'] = nrm((N_MOE, D, N_EXPERTS), D ** -0.5)
    inp['moe_w_gate'] = nrm((N_MOE, N_EXPERTS, D, D_FF_EXPERT), D ** -0.5)
    inp['moe_w_up'] = nrm((N_MOE, N_EXPERTS, D, D_FF_EXPERT), D ** -0.5)
    inp['moe_w_down'] = nrm((N_MOE, N_EXPERTS, D_FF_EXPERT, D), D_FF_EXPERT ** -0.5)
    inp['g_final'] = gain((D,))
    return inp


def _stack(states, name):
    return jnp.stack([s[name] for s in states], axis=0)


def reference(x_prompt, x_sample, cache_mla_latent, cache_mla_rope, cache_fox_k, cache_fox_v, cache_fox_logf,
              state_gdn, state_gdn_conv, page_table, c_prompt, c_sample,
              w_ada, b_ada, g_norm1, g_norm2, w_in, gdn_conv_w, gdn_a_log, gdn_dt_bias, gdn_norm_g,
              mla_q_norm_g, mla_w_qb, mla_kv_norm_g, mla_w_kvb, fox_f_bias, w_out,
              ffn_w_gate, ffn_w_up, ffn_w_down, moe_router, moe_w_gate, moe_w_up, moe_w_down, g_final):
    p = dict(w_ada=w_ada, b_ada=b_ada, g_norm1=g_norm1, g_norm2=g_norm2, w_in=w_in,
             gdn_conv_w=gdn_conv_w, gdn_a_log=gdn_a_log, gdn_dt_bias=gdn_dt_bias, gdn_norm_g=gdn_norm_g,
             mla_q_norm_g=mla_q_norm_g, mla_w_qb=mla_w_qb, mla_kv_norm_g=mla_kv_norm_g, mla_w_kvb=mla_w_kvb,
             fox_f_bias=fox_f_bias, w_out=w_out, ffn_w_gate=ffn_w_gate, ffn_w_up=ffn_w_up,
             ffn_w_down=ffn_w_down, moe_router=moe_router, moe_w_gate=moe_w_gate, moe_w_up=moe_w_up,
             moe_w_down=moe_w_down)
    xp, xs = x_prompt, x_sample
    sp, ss = [], []
    for l in range(DEPTH):
        past = dict(page_table=page_table, mla_lat=cache_mla_latent, mla_pe=cache_mla_rope,
                    fox_k=cache_fox_k, fox_v=cache_fox_v, fox_logf=cache_fox_logf,
                    gdn_state=state_gdn[l], gdn_conv=state_gdn_conv[l])
        xp, st_p = layer_step(xp, c_prompt, l, p, None)
        xs, st_s = layer_step(xs, c_sample, l, p, past)
        sp.append(st_p)
        ss.append(st_s)
    y_prompt = rmsnorm(xp, g_final)
    y_sample = rmsnorm(xs, g_final)
    return (y_prompt, y_sample,
            _stack(sp, 'mla_lat'), _stack(ss, 'mla_lat'),
            _stack(sp, 'mla_pe'), _stack(ss, 'mla_pe'),
            _stack(sp, 'fox_k'), _stack(ss, 'fox_k'),
            _stack(sp, 'fox_v'), _stack(ss, 'fox_v'),
            _stack(sp, 'fox_logf'), _stack(ss, 'fox_logf'),
            _stack(sp, 'gdn_state'), _stack(ss, 'gdn_state'),
            _stack(sp, 'gdn_conv'), _stack(ss, 'gdn_conv'))
```

```python
import functools
import math

import jax
import jax.numpy as jnp
from jax import lax
from jax.experimental import pallas as pl
from jax.experimental.pallas import tpu as pltpu

F32 = jnp.float32
BF16 = jnp.bfloat16
EPS = 1e-6
NEG = -1e30

GDN_H, GDN_DK, GDN_DV, GDN_CONV = 4, 128, 128, 4
GDN_QKV = GDN_H * (2 * GDN_DK + GDN_DV)
MLA_H, MLA_QR, MLA_KVR, MLA_NOPE, MLA_ROPE, MLA_V = 4, 256, 128, 64, 32, 64
FOX_H, FOX_HD = 4, 64
ROPE_BASE = 10000.0
MLA_SCALE = (MLA_NOPE + MLA_ROPE) ** -0.5
FOX_SCALE = FOX_HD ** -0.5

C_QKV, C_Z, C_CQ, C_QC, C_KC, C_VC, C_CKV, C_SM, N_Y = 0, 1536, 2048, 2304, 2560, 2816, 3072, 3200, 3328
SM_KPE, SM_KROT, SM_B, SM_A, SM_F = 0, 32, 64, 68, 72

V7X_VMEM_LIMIT = 52 * 1024 * 1024
ROW_TILE = 512


def _cparams(*sem):
    return pltpu.CompilerParams(dimension_semantics=sem, vmem_limit_bytes=V7X_VMEM_LIMIT)


def _dot(a, b):
    return jnp.dot(a, b, preferred_element_type=F32)


def _dot_nt(a, b):
    return lax.dot_general(a, b, (((1,), (1,)), ((), ())), preferred_element_type=F32)


def _split2(a):
    hi = a.astype(BF16)
    return hi, (a - hi.astype(F32)).astype(BF16)


def _split3(a):
    hi = a.astype(BF16)
    r = a - hi.astype(F32)
    mid = r.astype(BF16)
    return hi, mid, (r - mid.astype(F32)).astype(BF16)


def _dot_hi(a, b):
    ah, al = _split2(a)
    bh, bl = _split2(b)
    return _dot(ah, bh) + _dot(ah, bl) + _dot(al, bh)


def _dot_ones_lhs(ones_bf16, x):
    h, m, l = _split3(x)
    return _dot(ones_bf16, h) + _dot(ones_bf16, m) + _dot(ones_bf16, l)


def _dot_ones_rhs(x, ones_bf16):
    h, m, l = _split3(x)
    return _dot(h, ones_bf16) + _dot(m, ones_bf16) + _dot(l, ones_bf16)


def _sigmoid(x):
    return 1.0 / (1.0 + jnp.exp(-x))


def _silu(x):
    return x * _sigmoid(x)


def _softplus(x):
    return jnp.maximum(x, 0.0) + jnp.log(1.0 + jnp.exp(-jnp.abs(x)))


def _log_sigmoid(x):
    return -_softplus(-x)


def _rms(x):
    return x * lax.rsqrt(jnp.mean(x * x, axis=-1, keepdims=True) + EPS)


def _iota2(shape, dim):
    return lax.broadcasted_iota(jnp.int32, shape, dim)


def _ada_kernel(c_ref, w_ref, b_ref, o_ref):
    a = _silu(c_ref[...]).astype(BF16)
    o_ref[...] = _dot(a, w_ref[...].astype(BF16)) + b_ref[...]


def _adaln(c_all, w_ada, b_ada):
    n_layer, d, n = w_ada.shape
    r = c_all.shape[0]
    tn = 1536
    return pl.pallas_call(
        _ada_kernel,
        grid=(n_layer, n // tn),
        in_specs=[pl.BlockSpec((r, d), lambda l, j: (0, 0)),
                  pl.BlockSpec((None, d, tn), lambda l, j: (l, 0, j)),
                  pl.BlockSpec((None, 1, tn), lambda l, j: (l, 0, j))],
        out_specs=pl.BlockSpec((None, r, tn), lambda l, j: (l, 0, j)),
        out_shape=jax.ShapeDtypeStruct((n_layer, r, n), F32),
        compiler_params=_cparams("arbitrary", "arbitrary"),
        name="adaln",
    )(c_all, w_ada, b_ada.reshape(n_layer, 1, n))


def _mod_spec(tm, d, seq_len):
    if seq_len >= tm:
        return pl.BlockSpec((None, 1, d), lambda i, *_: ((i * tm) // seq_len, 0, 0))
    return pl.BlockSpec((tm, d), lambda i, *_: (i, 0))


def _in_proj_kernel(x_ref, g_ref, sc_ref, sh_ref, w_ref, o_ref):
    h = _rms(x_ref[...]) * g_ref[...]
    h = h * (1.0 + sc_ref[...]) + sh_ref[...]
    o_ref[...] = _dot(h.astype(BF16), w_ref[...])


def _in_proj(x, g, sc, sh, w, seq_len):
    r, d = x.shape
    n = w.shape[1]
    tm = 256
    ms = _mod_spec(tm, d, seq_len)
    return pl.pallas_call(
        _in_proj_kernel,
        grid=(r // tm,),
        in_specs=[pl.BlockSpec((tm, d), lambda i: (i, 0)),
                  pl.BlockSpec((1, d), lambda i: (0, 0)), ms, ms,
                  pl.BlockSpec((d, n), lambda i: (0, 0))],
        out_specs=pl.BlockSpec((tm, n), lambda i: (i, 0)),
        out_shape=jax.ShapeDtypeStruct((r, n), F32),
        compiler_params=_cparams("parallel"),
        name="in_proj",
    )(x, g, sc, sh, w)


def _gdn_prompt_kernel(qkv_ref, z_ref, sm_ref, cw_ref, alog_ref, dtb_ref, gn_ref, o_ref, so_ref,
                       ext_ref, s_ref, *, chunk, n_chunk):
    c = pl.program_id(1)
    n_h, dk = GDN_H, GDN_DK

    @pl.when(c == 0)
    def _():
        ext_ref[0:8, :] = jnp.zeros((8, GDN_QKV), F32)
        s_ref[...] = jnp.zeros_like(s_ref)

    ext_ref[8:8 + chunk, :] = qkv_ref[...]
    cw = cw_ref[...]
    conv = cw[0:1, :] * ext_ref[5:5 + chunk, :]
    for j in range(1, GDN_CONV):
        conv = conv + cw[j:j + 1, :] * ext_ref[5 + j:5 + j + chunk, :]
    ext_ref[0:8, :] = ext_ref[chunk:chunk + 8, :]
    act = _silu(conv)

    sm = sm_ref[...]
    g_log = -jnp.exp(alog_ref[...]) * _softplus(sm + dtb_ref[...])
    beta_all = _sigmoid(sm)
    row = _iota2((chunk, chunk), 0)
    col = _iota2((chunk, chunk), 1)
    incl = row >= col
    strict = row > col
    tri = jnp.where(incl, 1.0, 0.0).astype(BF16)
    eye = jnp.where(row == col, 1.0, 0.0)
    gc_all = _dot_ones_lhs(tri, g_log)
    gc_t = gc_all.T

    for h in range(n_h):
        q = act[:, h * dk:(h + 1) * dk]
        k = act[:, (n_h + h) * dk:(n_h + h + 1) * dk]
        v = act[:, 2 * n_h * dk + h * GDN_DV:2 * n_h * dk + (h + 1) * GDN_DV]
        q = q * lax.rsqrt(jnp.sum(q * q, axis=-1, keepdims=True) + EPS) * (dk ** -0.5)
        k = k * lax.rsqrt(jnp.sum(k * k, axis=-1, keepdims=True) + EPS)
        beta = beta_all[:, SM_B + h:SM_B + h + 1]
        gcol = gc_all[:, SM_A + h:SM_A + h + 1]
        grow = gc_t[SM_A + h:SM_A + h + 1, :]
        decay = jnp.where(incl, jnp.exp(jnp.where(incl, gcol - grow, 0.0)), 0.0)
        kb = k * beta
        vb = v * beta
        kbf = k.astype(BF16)
        x = -jnp.where(strict, _dot_nt(kb.astype(BF16), kbf) * decay, 0.0)
        minv = eye + x
        p = x
        for _ in range(int(math.log2(chunk)) - 1):
            p = _dot_hi(p, p)
            minv = minv + _dot_hi(minv, p)
        u = _dot_hi(minv, vb)
        w = _dot_hi(minv, kb * jnp.exp(gcol))
        aqk = _dot_nt(q.astype(BF16), kbf) * decay
        s_h = s_ref[h]
        s_bf = s_h.astype(BF16)
        v_new = u - _dot(w.astype(BF16), s_bf)
        o = _dot((q * jnp.exp(gcol)).astype(BF16), s_bf) + _dot(aqk.astype(BF16), v_new.astype(BF16))
        g_last = gcol[chunk - 1:chunk, :]
        k_dec_t = (k * jnp.exp(g_last - gcol)).T
        s_ref[h] = s_h * jnp.exp(g_last) + _dot(k_dec_t.astype(BF16), v_new.astype(BF16))
        o = _rms(o) * gn_ref[...] * _silu(z_ref[:, h * GDN_DV:(h + 1) * GDN_DV])
        o_ref[:, h * GDN_DV:(h + 1) * GDN_DV] = o.astype(o_ref.dtype)

    @pl.when(c == n_chunk - 1)
    def _():
        so_ref[...] = s_ref[...]


def _gdn_prompt(y, cw, alog_row, dtb_row, gn, batch, seq_len):
    chunk = 128
    n_chunk = seq_len // chunk
    r = y.shape[0]
    kern = functools.partial(_gdn_prompt_kernel, chunk=chunk, n_chunk=n_chunk)
    return pl.pallas_call(
        kern,
        grid=(batch, n_chunk),
        in_specs=[pl.BlockSpec((chunk, GDN_QKV), lambda b, c: (b * n_chunk + c, C_QKV // GDN_QKV)),
                  pl.BlockSpec((chunk, 512), lambda b, c: (b * n_chunk + c, C_Z // 512)),
                  pl.BlockSpec((chunk, 128), lambda b, c: (b * n_chunk + c, C_SM // 128)),
                  pl.BlockSpec((GDN_CONV, GDN_QKV), lambda b, c: (0, 0)),
                  pl.BlockSpec((1, 128), lambda b, c: (0, 0)),
                  pl.BlockSpec((1, 128), lambda b, c: (0, 0)),
                  pl.BlockSpec((1, GDN_DV), lambda b, c: (0, 0))],
        out_specs=[pl.BlockSpec((chunk, GDN_H * GDN_DV), lambda b, c: (b * n_chunk + c, 0)),
                   pl.BlockSpec((None, GDN_H, GDN_DK, GDN_DV), lambda b, c: (b, 0, 0, 0))],
        out_shape=[jax.ShapeDtypeStruct((r, GDN_H * GDN_DV), BF16),
                   jax.ShapeDtypeStruct((batch, GDN_H, GDN_DK, GDN_DV), F32)],
        scratch_shapes=[pltpu.VMEM((chunk + 8, GDN_QKV), F32),
                        pltpu.VMEM((GDN_H, GDN_DK, GDN_DV), F32)],
        compiler_params=_cparams("arbitrary", "arbitrary"),
        name="gdn_prompt",
    )(y, y, y, cw, alog_row, dtb_row, gn)


def _gdn_decode_kernel(qkv_ref, z_ref, sm_ref, buf_ref, s0_ref, cw_ref, alog_ref, dtb_ref, gn_ref,
                       o_ref, so_ref, ext_ref, qk_ref, *, n_tok):
    n_h, dk = GDN_H, GDN_DK
    ext_ref[0:GDN_CONV - 1, :] = buf_ref[...]
    ext_ref[GDN_CONV - 1:GDN_CONV - 1 + n_tok, :] = qkv_ref[...]
    cw = cw_ref[...]
    conv = cw[0:1, :] * ext_ref[0:n_tok, :]
    for j in range(1, GDN_CONV):
        conv = conv + cw[j:j + 1, :] * ext_ref[j:j + n_tok, :]
    act = _silu(conv)
    sm = sm_ref[...]
    decay_all = jnp.exp(-jnp.exp(alog_ref[...]) * _softplus(sm + dtb_ref[...]))
    beta_all = _sigmoid(sm)

    qk_ref[...] = jnp.zeros_like(qk_ref)
    for h in range(n_h):
        q = act[:, h * dk:(h + 1) * dk]
        k = act[:, (n_h + h) * dk:(n_h + h + 1) * dk]
        q = q * lax.rsqrt(jnp.sum(q * q, axis=-1, keepdims=True) + EPS) * (dk ** -0.5)
        k = k * lax.rsqrt(jnp.sum(k * k, axis=-1, keepdims=True) + EPS)
        qk_ref[h * n_tok:(h + 1) * n_tok, :] = q
        qk_ref[(n_h + h) * n_tok:(n_h + h + 1) * n_tok, :] = k
    qk_t = qk_ref[...].T

    for h in range(n_h):
        s = s0_ref[h]
        for t in range(n_tok):
            q_col = qk_t[:, h * n_tok + t:h * n_tok + t + 1]
            k_col = qk_t[:, (n_h + h) * n_tok + t:(n_h + h) * n_tok + t + 1]
            v_row = act[t:t + 1, 2 * n_h * dk + h * GDN_DV:2 * n_h * dk + (h + 1) * GDN_DV]
            s = s * decay_all[t:t + 1, SM_A + h:SM_A + h + 1]
            pred = jnp.sum(k_col * s, axis=0, keepdims=True)
            v_new = beta_all[t:t + 1, SM_B + h:SM_B + h + 1] * (v_row - pred)
            s = s + k_col * v_new
            o = jnp.sum(q_col * s, axis=0, keepdims=True)
            o = _rms(o) * gn_ref[...] * _silu(z_ref[t:t + 1, h * GDN_DV:(h + 1) * GDN_DV])
            o_ref[t:t + 1, h * GDN_DV:(h + 1) * GDN_DV] = o.astype(o_ref.dtype)
        so_ref[h] = s


def _gdn_decode(y3, buf, s0, cw, alog_row, dtb_row, gn):
    batch, n_tok, _ = y3.shape
    kern = functools.partial(_gdn_decode_kernel, n_tok=n_tok)
    return pl.pallas_call(
        kern,
        grid=(batch,),
        in_specs=[pl.BlockSpec((None, n_tok, GDN_QKV), lambda b: (b, 0, C_QKV // GDN_QKV)),
                  pl.BlockSpec((None, n_tok, 512), lambda b: (b, 0, C_Z // 512)),
                  pl.BlockSpec((None, n_tok, 128), lambda b: (b, 0, C_SM // 128)),
                  pl.BlockSpec((None, GDN_CONV - 1, GDN_QKV), lambda b: (b, 0, 0)),
                  pl.BlockSpec((None, GDN_H, GDN_DK, GDN_DV), lambda b: (b, 0, 0, 0)),
                  pl.BlockSpec((GDN_CONV, GDN_QKV), lambda b: (0, 0)),
                  pl.BlockSpec((1, 128), lambda b: (0, 0)),
                  pl.BlockSpec((1, 128), lambda b: (0, 0)),
                  pl.BlockSpec((1, GDN_DV), lambda b: (0, 0))],
        out_specs=[pl.BlockSpec((None, n_tok, GDN_H * GDN_DV), lambda b: (b, 0, 0)),
                   pl.BlockSpec((None, GDN_H, GDN_DK, GDN_DV), lambda b: (b, 0, 0, 0))],
        out_shape=[jax.ShapeDtypeStruct((batch, n_tok, GDN_H * GDN_DV), F32),
                   jax.ShapeDtypeStruct((batch, GDN_H, GDN_DK, GDN_DV), F32)],
        scratch_shapes=[pltpu.VMEM((16, GDN_QKV), F32), pltpu.VMEM((128, GDN_DK), F32)],
        compiler_params=_cparams("arbitrary"),
        name="gdn_decode",
    )(y3, y3, y3, buf, s0, cw, alog_row, dtb_row, gn)


def _mla_prep_kernel(cq_ref, ckv_ref, sm_ref, cs_ref, gq_ref, gkv_ref, wq_ref, wkb_ref,
                     qcat_ref, kcat_ref, lat_ref, kpe_ref):
    tm = cq_ref.shape[0]
    cs = cs_ref[...]
    cos = cs[:, 0:MLA_ROPE]
    sin = cs[:, MLA_ROPE:2 * MLA_ROPE]
    lat = _rms(ckv_ref[...]) * gkv_ref[...]
    sm = sm_ref[...]
    kpe = sm[:, SM_KPE:SM_KPE + MLA_ROPE] * cos + sm[:, SM_KROT:SM_KROT + MLA_ROPE] * sin
    lat_ref[...] = lat
    kpe_ref[...] = kpe
    pad = jnp.zeros((tm, 256 - MLA_KVR - MLA_ROPE), F32)
    kcat_ref[:, 0:MLA_KVR] = lat.astype(kcat_ref.dtype)
    kcat_ref[:, MLA_KVR:MLA_KVR + MLA_ROPE] = kpe.astype(kcat_ref.dtype)
    kcat_ref[:, MLA_KVR + MLA_ROPE:256] = pad.astype(kcat_ref.dtype)

    qn = _rms(cq_ref[...]) * gq_ref[...]
    qb = _dot(qn.astype(BF16), wq_ref[...])
    for h in range(MLA_H):
        base = h * 128
        nope = qb[:, base:base + MLA_NOPE]
        pe = (qb[:, base + MLA_NOPE:base + MLA_NOPE + MLA_ROPE] * cos
              + qb[:, base + MLA_NOPE + MLA_ROPE:base + 128] * sin)
        q_abs = _dot(nope.astype(BF16), wkb_ref[h])
        qcat_ref[h, :, 0:MLA_KVR] = (q_abs * MLA_SCALE).astype(qcat_ref.dtype)
        qcat_ref[h, :, MLA_KVR:MLA_KVR + MLA_ROPE] = (pe * MLA_SCALE).astype(qcat_ref.dtype)
        qcat_ref[h, :, MLA_KVR + MLA_ROPE:256] = pad.astype(qcat_ref.dtype)


def _mla_prep(y, cs, gq, gkv, wq, wkb):
    r = y.shape[0]
    tm = ROW_TILE
    n_tab = cs.shape[0] // tm
    return pl.pallas_call(
        _mla_prep_kernel,
        grid=(r // tm,),
        in_specs=[pl.BlockSpec((tm, MLA_QR), lambda i: (i, C_CQ // MLA_QR)),
                  pl.BlockSpec((tm, MLA_KVR), lambda i: (i, C_CKV // MLA_KVR)),
                  pl.BlockSpec((tm, 128), lambda i: (i, C_SM // 128)),
                  pl.BlockSpec((tm, 2 * MLA_ROPE), lambda i: (i % n_tab, 0)),
                  pl.BlockSpec((1, MLA_QR), lambda i: (0, 0)),
                  pl.BlockSpec((1, MLA_KVR), lambda i: (0, 0)),
                  pl.BlockSpec((MLA_QR, MLA_H * 128), lambda i: (0, 0)),
                  pl.BlockSpec((MLA_H, MLA_NOPE, MLA_KVR), lambda i: (0, 0, 0))],
        out_specs=[pl.BlockSpec((MLA_H, tm, 256), lambda i: (0, i, 0)),
                   pl.BlockSpec((tm, 256), lambda i: (i, 0)),
                   pl.BlockSpec((tm, MLA_KVR), lambda i: (i, 0)),
                   pl.BlockSpec((tm, MLA_ROPE), lambda i: (i, 0))],
        out_shape=[jax.ShapeDtypeStruct((MLA_H, r, 256), BF16),
                   jax.ShapeDtypeStruct((r, 256), BF16),
                   jax.ShapeDtypeStruct((r, MLA_KVR), F32),
                   jax.ShapeDtypeStruct((r, MLA_ROPE), F32)],
        compiler_params=_cparams("parallel"),
        name="mla_prep",
    )(y, y, y, cs, gq, gkv, wq, wkb)


def _fox_prep_kernel(*refs, seq_len, tm, emit_heads):
    if emit_heads:
        qc_ref, kc_ref, vc_ref, sm_ref, fb_ref, q_ref, k_ref, v_ref, fx_ref, carry_ref = refs
    else:
        sm_ref, fb_ref, fx_ref, carry_ref = refs
    i = pl.program_id(0)
    logf = _log_sigmoid(sm_ref[...] + fb_ref[...])
    row = _iota2((tm, tm), 0)
    col = _iota2((tm, tm), 1)
    if seq_len >= tm:
        @pl.when((i * tm) % seq_len == 0)
        def _():
            carry_ref[...] = jnp.zeros_like(carry_ref)
        tri = jnp.where(row >= col, 1.0, 0.0).astype(BF16)
        g = _dot_ones_lhs(tri, logf) + carry_ref[...]
        carry_ref[...] = g[tm - 1:tm, :]
    else:
        same = (row // seq_len) == (col // seq_len)
        tri = jnp.where((row >= col) & same, 1.0, 0.0).astype(BF16)
        g = _dot_ones_lhs(tri, logf)
    fx_ref[:, 0:128] = logf
    fx_ref[:, 128:256] = g
    if emit_heads:
        for h in range(FOX_H):
            sl = slice(h * FOX_HD, (h + 1) * FOX_HD)
            q_ref[h] = (qc_ref[:, sl] * FOX_SCALE).astype(BF16)
            k_ref[h] = kc_ref[:, sl].astype(BF16)
            v_ref[h] = vc_ref[:, sl].astype(BF16)


def _fox_prep(y, fb_row, seq_len, emit_heads):
    r = y.shape[0]
    tm = ROW_TILE
    kern = functools.partial(_fox_prep_kernel, seq_len=seq_len, tm=tm, emit_heads=emit_heads)
    sm_spec = pl.BlockSpec((tm, 128), lambda i: (i, C_SM // 128))
    fb_spec = pl.BlockSpec((1, 128), lambda i: (0, 0))
    fx_spec = pl.BlockSpec((tm, 256), lambda i: (i, 0))
    fx_shape = jax.ShapeDtypeStruct((r, 256), F32)
    w = FOX_H * FOX_HD
    if emit_heads:
        head_spec = pl.BlockSpec((FOX_H, tm, FOX_HD), lambda i: (0, i, 0))
        head_shape = jax.ShapeDtypeStruct((FOX_H, r, FOX_HD), BF16)
        in_specs = [pl.BlockSpec((tm, w), lambda i: (i, C_QC // w)),
                    pl.BlockSpec((tm, w), lambda i: (i, C_KC // w)),
                    pl.BlockSpec((tm, w), lambda i: (i, C_VC // w)), sm_spec, fb_spec]
        args = (y, y, y, y, fb_row)
        out_specs = [head_spec, head_spec, head_spec, fx_spec]
        out_shape = [head_shape, head_shape, head_shape, fx_shape]
    else:
        in_specs = [sm_spec, fb_spec]
        args = (y, fb_row)
        out_specs = [fx_spec]
        out_shape = [fx_shape]
    return pl.pallas_call(
        kern, grid=(r // tm,), in_specs=in_specs, out_specs=out_specs, out_shape=out_shape,
        scratch_shapes=[pltpu.VMEM((1, 128), F32)],
        compiler_params=_cparams("arbitrary"),
        name="fox_prep",
    )(*args)


def _flash_kernel(*refs, n_head, shared_kv, has_bias, has_proj, d_v, d_out):
    refs = list(refs)
    q_ref, k_ref, v_ref = refs[:3]
    pos = 3
    if has_bias:
        gq_ref, gk_ref = refs[pos:pos + 2]
        pos += 2
    if has_proj:
        wp_ref = refs[pos]
        pos += 1
    o_ref, m_ref, l_ref, acc_ref = refs[pos:pos + 4]
    qi = pl.program_id(1)
    ki = pl.program_id(2)
    tq = q_ref.shape[1]
    tk = k_ref.shape[-2]

    @pl.when(ki == 0)
    def _():
        m_ref[...] = jnp.full(m_ref.shape, NEG, F32)
        l_ref[...] = jnp.zeros_like(l_ref)
        acc_ref[...] = jnp.zeros_like(acc_ref)

    def update(masked):
        for h in range(n_head):
            k = k_ref[...] if shared_kv else k_ref[h]
            v = v_ref[...] if shared_kv else v_ref[h]
            s = _dot_nt(q_ref[h], k)
            if has_bias:
                s = s + gq_ref[h] - gk_ref[h]
            if masked:
                s = jnp.where(_iota2((tq, tk), 1) <= _iota2((tq, tk), 0), s, NEG)
            m_prev = m_ref[h]
            m_new = jnp.maximum(m_prev, jnp.max(s, axis=-1, keepdims=True))
            alpha = jnp.exp(m_prev - m_new)
            p = jnp.exp(s - m_new)
            l_ref[h] = alpha * l_ref[h] + jnp.sum(p, axis=-1, keepdims=True)
            acc_ref[h] = alpha * acc_ref[h] + _dot(p.astype(BF16), v)
            m_ref[h] = m_new

    @pl.when(ki < qi)
    def _():
        update(False)

    @pl.when(ki == qi)
    def _():
        update(True)
        for h in range(n_head):
            o = acc_ref[h] / l_ref[h]
            if has_proj:
                o = _dot(o.astype(BF16), wp_ref[h])
            o_ref[:, h * d_out:(h + 1) * d_out] = o.astype(o_ref.dtype)


def _flash(q, k, v, batch, seq_len, *, shared_kv, gq=None, gk=None, wp=None, d_v, d_out):
    n_head, r, d_k = q.shape
    t = ROW_TILE
    nb = seq_len // t
    has_bias = gq is not None
    has_proj = wp is not None
    kern = functools.partial(_flash_kernel, n_head=n_head, shared_kv=shared_kv, has_bias=has_bias,
                             has_proj=has_proj, d_v=d_v, d_out=d_out)
    q_spec = pl.BlockSpec((n_head, t, d_k), lambda b, i, j: (0, b * nb + i, 0))
    if shared_kv:
        k_spec = pl.BlockSpec((t, d_k), lambda b, i, j: (b * nb + jnp.minimum(i, j), 0))
        v_spec = pl.BlockSpec((t, d_v), lambda b, i, j: (b * nb + jnp.minimum(i, j), 0))
    else:
        k_spec = pl.BlockSpec((n_head, t, d_k), lambda b, i, j: (0, b * nb + jnp.minimum(i, j), 0))
        v_spec = pl.BlockSpec((n_head, t, d_v), lambda b, i, j: (0, b * nb + jnp.minimum(i, j), 0))
    in_specs = [q_spec, k_spec, v_spec]
    args = [q, k, v]
    if has_bias:
        in_specs += [pl.BlockSpec((n_head, t, 1), lambda b, i, j: (0, b * nb + i, 0)),
                     pl.BlockSpec((n_head, 1, t), lambda b, i, j: (0, 0, b * nb + jnp.minimum(i, j)))]
        args += [gq, gk]
    if has_proj:
        in_specs.append(pl.BlockSpec(wp.shape, lambda b, i, j: (0, 0, 0)))
        args.append(wp)
    return pl.pallas_call(
        kern,
        grid=(batch, nb, nb),
        in_specs=in_specs,
        out_specs=pl.BlockSpec((t, n_head * d_out), lambda b, i, j: (b * nb + i, 0)),
        out_shape=jax.ShapeDtypeStruct((r, n_head * d_out), BF16),
        scratch_shapes=[pltpu.VMEM((n_head, t, 1), F32), pltpu.VMEM((n_head, t, 1), F32),
                        pltpu.VMEM((n_head, t, d_v), F32)],
        compiler_params=_cparams("parallel", "arbitrary", "arbitrary"),
        name="flash_shared" if shared_kv else "flash_heads",
    )(*args)


PAGES_PER_STEP = 8


def _softmax_step(s, m_ref, l_ref):
    m_prev = m_ref[...]
    m_new = jnp.maximum(m_prev, jnp.max(s, axis=-1, keepdims=True))
    alpha = jnp.exp(m_prev - m_new)
    p = jnp.exp(s - m_new)
    l_ref[...] = alpha * l_ref[...] + jnp.sum(p, axis=-1, keepdims=True)
    m_ref[...] = m_new
    return alpha, p


def _new_token_mask(n_rows, n_keys, n_tok, n_head):
    t = _iota2((n_rows, n_keys), 0) // n_head
    j = _iota2((n_rows, n_keys), 1)
    return (j <= t) & (j < n_tok)


def _mla_decode_kernel(pt_ref, q_ref, knew_ref, wvb_ref, *refs, pps, n_step, n_tok):
    lat_refs = refs[:pps]
    pe_refs = refs[pps:2 * pps]
    o_ref, m_ref, l_ref, acc_ref = refs[2 * pps:2 * pps + 4]
    c = pl.program_id(1)
    n_rows = q_ref.shape[0]

    @pl.when(c == 0)
    def _():
        m_ref[...] = jnp.full(m_ref.shape, NEG, F32)
        l_ref[...] = jnp.zeros_like(l_ref)
        acc_ref[...] = jnp.zeros_like(acc_ref)

    q = q_ref[...]
    q_abs = q[:, 0:MLA_KVR]
    q_pe = q[:, MLA_KVR:MLA_KVR + MLA_ROPE]
    lats = [lat_refs[j][...].astype(BF16) for j in range(pps)]
    s = jnp.concatenate(
        [_dot_nt(q_abs, lats[j]) + _dot_nt(q_pe, pe_refs[j][...].astype(BF16)) for j in range(pps)], axis=-1)
    alpha, p = _softmax_step(s, m_ref, l_ref)
    acc = alpha * acc_ref[...]
    for j in range(pps):
        acc = acc + _dot(p[:, j * 128:(j + 1) * 128].astype(BF16), lats[j])
    acc_ref[...] = acc

    @pl.when(c == n_step - 1)
    def _():
        kn = knew_ref[...]
        s_new = _dot_nt(q, kn)
        s_new = jnp.where(_new_token_mask(n_rows, kn.shape[0], n_tok, MLA_H), s_new, NEG)
        alpha2, p2 = _softmax_step(s_new, m_ref, l_ref)
        o = (alpha2 * acc_ref[...] + _dot(p2.astype(BF16), kn[:, 0:MLA_KVR])) / l_ref[...]
        full = _dot(o.astype(BF16), wvb_ref[...])
        head = _iota2((n_rows, MLA_V), 0) % MLA_H
        out = jnp.zeros((n_rows, MLA_V), F32)
        for h in range(MLA_H):
            out = out + jnp.where(head == h, full[:, h * MLA_V:(h + 1) * MLA_V], 0.0)
        o_ref[...] = out.astype(o_ref.dtype)


def _mla_decode(page_table, q, knew, wvb, cache_lat, cache_pe, layer, n_tok):
    batch, n_rows, _ = q.shape
    n_pages = page_table.shape[1]
    pps = PAGES_PER_STEP
    n_step = n_pages // pps
    page = cache_lat.shape[2]
    kern = functools.partial(_mla_decode_kernel, pps=pps, n_step=n_step, n_tok=n_tok)

    def page_spec(width, j):
        return pl.BlockSpec((None, None, page, width), lambda b, c, pt: (layer, pt[b, c * pps + j], 0, 0))

    in_specs = ([pl.BlockSpec((None, n_rows, 256), lambda b, c, pt: (b, 0, 0)),
                 pl.BlockSpec((None, knew.shape[1], 256), lambda b, c, pt: (b, 0, 0)),
                 pl.BlockSpec(wvb.shape, lambda b, c, pt: (0, 0))]
                + [page_spec(MLA_KVR, j) for j in range(pps)]
                + [page_spec(MLA_ROPE, j) for j in range(pps)])
    return pl.pallas_call(
        kern,
        grid_spec=pltpu.PrefetchScalarGridSpec(
            num_scalar_prefetch=1, grid=(batch, n_step), in_specs=in_specs,
            out_specs=pl.BlockSpec((None, n_rows, MLA_V), lambda b, c, pt: (b, 0, 0)),
            scratch_shapes=[pltpu.VMEM((n_rows, 1), F32), pltpu.VMEM((n_rows, 1), F32),
                            pltpu.VMEM((n_rows, MLA_KVR), F32)]),
        out_shape=jax.ShapeDtypeStruct((batch, n_rows, MLA_V), BF16),
        compiler_params=_cparams("arbitrary", "arbitrary"),
        name="mla_decode",
    )(page_table, q, knew, wvb, *([cache_lat] * pps), *([cache_pe] * pps))


def _fox_decode_kernel(pt_ref, q_ref, g_ref, knew_ref, vnew_ref, *refs, pps, n_step, n_tok):
    k_refs = refs[:pps]
    v_refs = refs[pps:2 * pps]
    lp_refs = refs[2 * pps:3 * pps]
    o_ref, m_ref, l_ref, acc_ref, tot_ref = refs[3 * pps:3 * pps + 5]
    c = pl.program_id(1)
    n_rows = q_ref.shape[0]
    n_h, hd = FOX_H, FOX_HD
    page = lp_refs[0].shape[-1]

    @pl.when(c == 0)
    def _():
        m_ref[...] = jnp.full(m_ref.shape, NEG, F32)
        l_ref[...] = jnp.zeros_like(l_ref)
        acc_ref[...] = jnp.zeros_like(acc_ref)
        tot_ref[...] = jnp.zeros_like(tot_ref)

    q = q_ref[...]
    later = jnp.where(_iota2((page, page), 0) >= _iota2((page, page), 1), 1.0, 0.0).astype(BF16)
    tot = tot_ref[...]
    parts = []
    for j in range(pps):
        lp = lp_refs[j][...]
        rev = _dot_ones_rhs(lp, later)
        suffix = tot + rev - lp
        tot = tot + rev[:, 0:1]
        s = _dot_nt(q[:, 0:hd], k_refs[j][:, 0, :].astype(BF16))
        for h in range(1, n_h):
            s = s + _dot_nt(q[:, h * hd:(h + 1) * hd], k_refs[j][:, h, :].astype(BF16))
        parts.append(s * FOX_SCALE + jnp.concatenate([suffix] * (n_rows // (2 * n_h)), axis=0) + g_ref[...])
    tot_ref[...] = tot
    alpha, p = _softmax_step(jnp.concatenate(parts, axis=-1), m_ref, l_ref)
    for h in range(n_h):
        acc = alpha * acc_ref[h]
        for j in range(pps):
            acc = acc + _dot(p[:, j * page:(j + 1) * page].astype(BF16), v_refs[j][:, h, :].astype(BF16))
        acc_ref[h] = acc

    @pl.when(c == n_step - 1)
    def _():
        kn = knew_ref[...]
        gcol = g_ref[...]
        s_new = _dot_nt(q, kn) * FOX_SCALE
        rows_i = _iota2((n_rows, kn.shape[0]), 0)
        cols_j = _iota2((n_rows, kn.shape[0]), 1)
        gk = jnp.zeros((n_rows, kn.shape[0]), F32)
        for j in range(n_tok):
            for h in range(n_h):
                gj = gcol[j * n_h + h:j * n_h + h + 1, :]
                gk = gk + jnp.where((cols_j == j) & (rows_i % n_h == h), gj, 0.0)
        s_new = s_new + gcol - gk
        s_new = jnp.where(_new_token_mask(n_rows, kn.shape[0], n_tok, n_h), s_new, NEG)
        alpha2, p2 = _softmax_step(s_new, m_ref, l_ref)
        full = _dot(p2.astype(BF16), vnew_ref[...])
        head = _iota2((n_rows, hd), 0) % n_h
        out = jnp.zeros((n_rows, hd), F32)
        for h in range(n_h):
            o_h = alpha2 * acc_ref[h] + full[:, h * hd:(h + 1) * hd]
            out = out + jnp.where(head == h, o_h, 0.0)
        o_ref[...] = (out / l_ref[...]).astype(o_ref.dtype)


def _fox_decode(page_table, q, gcol, knew, vnew, cache_k, cache_v, lp8, layer, n_tok):
    batch, n_rows, _ = q.shape
    n_pages = page_table.shape[1]
    pps = PAGES_PER_STEP
    n_step = n_pages // pps
    page = cache_k.shape[2]
    kern = functools.partial(_fox_decode_kernel, pps=pps, n_step=n_step, n_tok=n_tok)

    def kv_spec(j):
        return pl.BlockSpec((None, None, page, FOX_H, FOX_HD),
                            lambda b, c, pt: (layer, pt[b, n_pages - 1 - (c * pps + j)], 0, 0, 0))

    def lp_spec(j):
        return pl.BlockSpec((None, None, 2 * FOX_H, page),
                            lambda b, c, pt: (layer, pt[b, n_pages - 1 - (c * pps + j)], 0, 0))

    w = FOX_H * FOX_HD
    in_specs = ([pl.BlockSpec((None, n_rows, w), lambda b, c, pt: (b, 0, 0)),
                 pl.BlockSpec((None, n_rows, 1), lambda b, c, pt: (b, 0, 0)),
                 pl.BlockSpec((None, knew.shape[1], w), lambda b, c, pt: (b, 0, 0)),
                 pl.BlockSpec((None, vnew.shape[1], w), lambda b, c, pt: (b, 0, 0))]
                + [kv_spec(j) for j in range(pps)] + [kv_spec(j) for j in range(pps)]
                + [lp_spec(j) for j in range(pps)])
    return pl.pallas_call(
        kern,
        grid_spec=pltpu.PrefetchScalarGridSpec(
            num_scalar_prefetch=1, grid=(batch, n_step), in_specs=in_specs,
            out_specs=pl.BlockSpec((None, n_rows, FOX_HD), lambda b, c, pt: (b, 0, 0)),
            scratch_shapes=[pltpu.VMEM((n_rows, 1), F32), pltpu.VMEM((n_rows, 1), F32),
                            pltpu.VMEM((FOX_H, n_rows, FOX_HD), F32), pltpu.VMEM((2 * FOX_H, page), F32)]),
        out_shape=jax.ShapeDtypeStruct((batch, n_rows, FOX_HD), BF16),
        compiler_params=_cparams("arbitrary", "arbitrary"),
        name="fox_decode",
    )(page_table, q, gcol, knew, vnew, *([cache_k] * pps), *([cache_v] * pps), *([lp8] * pps))


def _out_proj_kernel(oa_ref, ob_ref, oc_ref, w_ref, x_ref, gt_ref, o_ref):
    na, nb = oa_ref.shape[1], ob_ref.shape[1]
    acc = _dot(oa_ref[...].astype(BF16), w_ref[0:na, :])
    acc = acc + _dot(ob_ref[...].astype(BF16), w_ref[na:na + nb, :])
    acc = acc + _dot(oc_ref[...].astype(BF16), w_ref[na + nb:, :])
    o_ref[...] = x_ref[...] + (1.0 + gt_ref[...]) * acc


def _out_proj(oa, ob, oc, w, x, gt, seq_len):
    r, d = x.shape
    tm = ROW_TILE
    return pl.pallas_call(
        _out_proj_kernel,
        grid=(r // tm,),
        in_specs=[pl.BlockSpec((tm, oa.shape[1]), lambda i: (i, 0)),
                  pl.BlockSpec((tm, ob.shape[1]), lambda i: (i, 0)),
                  pl.BlockSpec((tm, oc.shape[1]), lambda i: (i, 0)),
                  pl.BlockSpec(w.shape, lambda i: (0, 0)),
                  pl.BlockSpec((tm, d), lambda i: (i, 0)),
                  _mod_spec(tm, d, seq_len)],
        out_specs=pl.BlockSpec((tm, d), lambda i: (i, 0)),
        out_shape=jax.ShapeDtypeStruct((r, d), F32),
        compiler_params=_cparams("parallel"),
        name="out_proj",
    )(oa, ob, oc, w, x, gt)


def _ffn_kernel(*refs, n_expert, routed, final_norm):
    refs = list(refs)
    x_ref, g_ref, sc_ref, sh_ref, gt_ref = refs[:5]
    pos = 5
    if routed:
        rt_ref = refs[pos]
        pos += 1
    wg_ref, wu_ref, wd_ref = refs[pos:pos + 3]
    pos += 3
    if final_norm:
        gf_ref = refs[pos]
        pos += 1
    o_ref, h_ref, acc_ref = refs[pos:pos + 3]
    if routed:
        gates_ref = refs[pos + 3]
    e = pl.program_id(1)
    f = pl.program_id(2)
    tm = x_ref.shape[0]

    @pl.when((e == 0) & (f == 0))
    def _():
        h = _rms(x_ref[...]) * g_ref[...]
        h = h * (1.0 + sc_ref[...]) + sh_ref[...]
        h_ref[...] = h.astype(BF16)
        acc_ref[...] = jnp.zeros_like(acc_ref)
        if routed:
            lane = _iota2((tm, 128), 1)
            logits = jnp.where(lane < n_expert, _dot_hi(h, rt_ref[...]), NEG)
            m1 = jnp.max(logits, axis=-1, keepdims=True)
            i1 = jnp.min(jnp.where(logits == m1, lane, 128), axis=-1, keepdims=True)
            rest = jnp.where(lane == i1, NEG, logits)
            m2 = jnp.max(rest, axis=-1, keepdims=True)
            i2 = jnp.min(jnp.where(rest == m2, lane, 128), axis=-1, keepdims=True)
            e2 = jnp.exp(m2 - m1)
            w1 = 1.0 / (1.0 + e2)
            gates_ref[...] = jnp.where(lane == i1, w1, 0.0) + jnp.where(lane == i2, e2 * w1, 0.0)

    hb = h_ref[...]
    a = _dot(hb, wg_ref[...])
    u = _dot(hb, wu_ref[...])
    y = _dot((_silu(a) * u).astype(BF16), wd_ref[...])
    if routed:
        lane = _iota2((tm, 128), 1)
        y = y * jnp.sum(jnp.where(lane == e, gates_ref[...], 0.0), axis=-1, keepdims=True)
    acc_ref[...] += y

    @pl.when((e == n_expert - 1) & (f == pl.num_programs(2) - 1))
    def _():
        out = x_ref[...] + (1.0 + gt_ref[...]) * acc_ref[...]
        if final_norm:
            out = _rms(out) * gf_ref[...]
        o_ref[...] = out


def _ffn(x, g, sc, sh, gt, wg, wu, wd, seq_len, router=None, g_final=None):
    r, d = x.shape
    n_expert, _, d_ff = wg.shape
    tm = ROW_TILE
    tf = d_ff // 2
    routed = router is not None
    final_norm = g_final is not None
    ms = _mod_spec(tm, d, seq_len)
    in_specs = [pl.BlockSpec((tm, d), lambda i, e, f: (i, 0)),
                pl.BlockSpec((1, d), lambda i, e, f: (0, 0)), ms, ms, ms]
    args = [x, g, sc, sh, gt]
    if routed:
        in_specs.append(pl.BlockSpec(router.shape, lambda i, e, f: (0, 0)))
        args.append(router)
    in_specs += [pl.BlockSpec((None, d, tf), lambda i, e, f: (e, 0, f)),
                 pl.BlockSpec((None, d, tf), lambda i, e, f: (e, 0, f)),
                 pl.BlockSpec((None, tf, d), lambda i, e, f: (e, f, 0))]
    args += [wg, wu, wd]
    if final_norm:
        in_specs.append(pl.BlockSpec((1, d), lambda i, e, f: (0, 0)))
        args.append(g_final)
    scratch = [pltpu.VMEM((tm, d), BF16), pltpu.VMEM((tm, d), F32)]
    if routed:
        scratch.append(pltpu.VMEM((tm, 128), F32))
    kern = functools.partial(_ffn_kernel, n_expert=n_expert, routed=routed, final_norm=final_norm)
    return pl.pallas_call(
        kern,
        grid=(r // tm, n_expert, d_ff // tf),
        in_specs=in_specs,
        out_specs=pl.BlockSpec((tm, d), lambda i, e, f: (i, 0)),
        out_shape=jax.ShapeDtypeStruct((r, d), F32),
        scratch_shapes=scratch,
        compiler_params=_cparams("parallel", "arbitrary", "arbitrary"),
        name="moe_ffn" if routed else "dense_ffn",
    )(*args)


def _rot_cols(w):
    half = w.shape[-1] // 2
    return jnp.concatenate([-w[..., half:], w[..., :half]], axis=-1)


def _prep_w_in(w):
    d = w.shape[0]
    sizes = (GDN_QKV, GDN_H * GDN_DV, GDN_H, GDN_H, MLA_QR, MLA_KVR, MLA_ROPE,
             FOX_H * FOX_HD, FOX_H * FOX_HD, FOX_H * FOX_HD, FOX_H)
    pieces, off = [], 0
    for s in sizes:
        pieces.append(w[:, off:off + s])
        off += s
    qkv, z, b, a, cq, ckv, kpe, qc, kc, vc, f = pieces
    used = 2 * MLA_ROPE + 2 * GDN_H + FOX_H
    small = jnp.concatenate([kpe, _rot_cols(kpe), b, a, f, jnp.zeros((d, 128 - used), w.dtype)], axis=1)
    return jnp.concatenate([qkv, z, cq, qc, kc, vc, ckv, small], axis=1).astype(BF16)


def _prep_w_qb(w):
    per = MLA_NOPE + MLA_ROPE
    cols = []
    for h in range(MLA_H):
        nope = w[:, h * per:h * per + MLA_NOPE]
        pe = w[:, h * per + MLA_NOPE:(h + 1) * per]
        cols += [nope, pe, _rot_cols(pe)]
    return jnp.concatenate(cols, axis=1).astype(BF16)


def _lane_row(values, offset):
    row = jnp.zeros((1, 128), F32)
    return row.at[0, offset:offset + values.shape[0]].set(values.astype(F32))


def _rope_table(pos):
    half = MLA_ROPE // 2
    inv = jnp.exp(-math.log(ROPE_BASE) * jnp.arange(half, dtype=F32) / half)
    ang = pos.astype(F32)[:, None] * inv[None, :]
    cos, sin = jnp.cos(ang), jnp.sin(ang)
    return jnp.concatenate([cos, cos, sin, sin], axis=-1)


def kernel(x_prompt, x_sample, cache_mla_latent, cache_mla_rope, cache_fox_k, cache_fox_v, cache_fox_logf, state_gdn, state_gdn_conv, page_table, c_prompt, c_sample, w_ada, b_ada, g_norm1, g_norm2, w_in, gdn_conv_w, gdn_a_log, gdn_dt_bias, gdn_norm_g, mla_q_norm_g, mla_w_qb, mla_kv_norm_g, mla_w_kvb, fox_f_bias, w_out, ffn_w_gate, ffn_w_up, ffn_w_down, moe_router, moe_w_gate, moe_w_up, moe_w_down, g_final):
    n_layer = w_in.shape[0]
    bp, tp, d = x_prompt.shape
    bs, ts, _ = x_sample.shape
    n_pages, page = page_table.shape[1], cache_mla_latent.shape[2]
    past_len = n_pages * page
    assert tp % ROW_TILE == 0 and (bs * ts) % ROW_TILE == 0 and ROW_TILE % ts == 0
    assert tp >= GDN_CONV - 1 and ts >= GDN_CONV - 1 and n_pages % PAGES_PER_STEP == 0

    n_c = bp + bs
    c_all = jnp.concatenate([c_prompt, c_sample, jnp.zeros((-n_c % 8, d), F32)], axis=0)
    mod = _adaln(c_all, w_ada, b_ada)

    lp = jnp.swapaxes(cache_fox_logf, 2, 3)
    lp8 = jnp.concatenate([lp, lp], axis=2)

    cs_p = _rope_table(jnp.arange(tp))
    cs_s = jnp.tile(_rope_table(past_len + jnp.arange(ts)), (bs, 1))
    head_mask = (jnp.arange(FOX_H)[:, None] == (jnp.arange(FOX_H * FOX_HD) // FOX_HD)[None, :])

    xp = x_prompt.reshape(bp * tp, d)
    xs = x_sample.reshape(bs * ts, d)
    outs = {k: [] for k in ("lat_p", "lat_s", "pe_p", "pe_s", "fk_p", "fk_s", "fv_p", "fv_s",
                            "lf_p", "lf_s", "st_p", "st_s", "cv_p", "cv_s")}

    for l in range(n_layer):
        w_in_l = _prep_w_in(w_in[l])
        w_qb_l = _prep_w_qb(mla_w_qb[l])
        w_kvb = mla_w_kvb[l].reshape(MLA_KVR, MLA_H, MLA_NOPE + MLA_V)
        w_kb_t = jnp.transpose(w_kvb[..., :MLA_NOPE], (1, 2, 0)).astype(BF16)
        w_vb_h = jnp.transpose(w_kvb[..., MLA_NOPE:], (1, 0, 2)).astype(BF16)
        w_vb_all = w_kvb[..., MLA_NOPE:].reshape(MLA_KVR, MLA_H * MLA_V).astype(BF16)
        w_out_l = w_out[l].astype(BF16)
        alog_row = _lane_row(gdn_a_log[l], SM_A)
        dtb_row = _lane_row(gdn_dt_bias[l], SM_A)
        fb_row = _lane_row(fox_f_bias[l], SM_F)
        gn = gdn_norm_g[l].reshape(1, GDN_DV)
        gq = mla_q_norm_g[l].reshape(1, MLA_QR)
        gkv = mla_kv_norm_g[l].reshape(1, MLA_KVR)
        g1 = g_norm1[l].reshape(1, d)
        g2 = g_norm2[l].reshape(1, d)
        cw = gdn_conv_w[l]
        i_ffn = l // 2
        last = l == n_layer - 1

        mods_p = [m.reshape(bp, 1, d) for m in jnp.split(mod[l, :bp], 6, axis=-1)]
        mods_s = [jnp.repeat(m, ts, axis=0) for m in jnp.split(mod[l, bp:bp + bs], 6, axis=-1)]

        def mixer_tail(x, mods, oa, ob, oc, seq_len):
            x = _out_proj(oa, ob, oc, w_out_l, x, mods[2], seq_len)
            gf = g_final.reshape(1, d) if last else None
            if l % 2 == 0:
                return _ffn(x, g2, mods[4], mods[3], mods[5], ffn_w_gate[i_ffn][None].astype(BF16),
                            ffn_w_up[i_ffn][None].astype(BF16), ffn_w_down[i_ffn][None].astype(BF16),
                            seq_len, g_final=gf)
            router = jnp.pad(moe_router[i_ffn], ((0, 0), (0, 128 - moe_router.shape[-1])))
            return _ffn(x, g2, mods[4], mods[3], mods[5], moe_w_gate[i_ffn].astype(BF16),
                        moe_w_up[i_ffn].astype(BF16), moe_w_down[i_ffn].astype(BF16),
                        seq_len, router=router, g_final=gf)

        y = _in_proj(xp, g1, mods_p[1], mods_p[0], w_in_l, tp)
        oa, st = _gdn_prompt(y, cw, alog_row, dtb_row, gn, bp, tp)
        qcat, kcat, lat, kpe = _mla_prep(y, cs_p, gq, gkv, w_qb_l, w_kb_t)
        ob = _flash(qcat, kcat, kcat, bp, tp, shared_kv=True, wp=w_vb_h, d_v=MLA_KVR, d_out=MLA_V)
        fq, fk, fv, fx = _fox_prep(y, fb_row, tp, True)
        g_cum = fx[:, 128 + SM_F:128 + SM_F + FOX_H]
        oc = _flash(fq, fk, fv, bp, tp, shared_kv=False, gq=g_cum.T[:, :, None], gk=g_cum.T[:, None, :],
                    d_v=FOX_HD, d_out=FOX_HD)
        y3 = y.reshape(bp, tp, N_Y)
        outs["lat_p"].append(lat.reshape(bp, tp, MLA_KVR))
        outs["pe_p"].append(kpe.reshape(bp, tp, MLA_ROPE))
        outs["fk_p"].append(y3[:, :, C_KC:C_KC + FOX_H * FOX_HD].reshape(bp, tp, FOX_H, FOX_HD))
        outs["fv_p"].append(y3[:, :, C_VC:C_VC + FOX_H * FOX_HD].reshape(bp, tp, FOX_H, FOX_HD))
        outs["lf_p"].append(fx[:, SM_F:SM_F + FOX_H].reshape(bp, tp, FOX_H))
        outs["st_p"].append(st)
        outs["cv_p"].append(y3[:, tp - (GDN_CONV - 1):, C_QKV:C_QKV + GDN_QKV])
        xp = mixer_tail(xp, mods_p, oa, ob, oc, tp)

        y = _in_proj(xs, g1, mods_s[1], mods_s[0], w_in_l, ts)
        y3 = y.reshape(bs, ts, N_Y)
        oa, st = _gdn_decode(y3, state_gdn_conv[l], state_gdn[l], cw, alog_row, dtb_row, gn)
        qcat, kcat, lat, kpe = _mla_prep(y, cs_s, gq, gkv, w_qb_l, w_kb_t)
        q_rows = jnp.transpose(qcat.reshape(MLA_H, bs, ts, 256), (1, 2, 0, 3)).reshape(bs, ts * MLA_H, 256)
        knew = jnp.pad(kcat.reshape(bs, ts, 256), ((0, 0), (0, 128 - ts), (0, 0)))
        ob = _mla_decode(page_table, q_rows, knew, w_vb_all, cache_mla_latent, cache_mla_rope, l, ts)
        (fx,) = _fox_prep(y, fb_row, ts, False)
        g_rows = fx[:, 128 + SM_F:128 + SM_F + FOX_H].reshape(bs, ts * FOX_H, 1)
        qc3 = y3[:, :, C_QC:C_QC + FOX_H * FOX_HD]
        q_bd = jnp.where(head_mask[None, None], qc3[:, :, None, :], 0.0).reshape(bs, ts * FOX_H, FOX_H * FOX_HD)
        kc3 = y3[:, :, C_KC:C_KC + FOX_H * FOX_HD]
        vc3 = y3[:, :, C_VC:C_VC + FOX_H * FOX_HD]
        pad_new = ((0, 0), (0, 128 - ts), (0, 0))
        oc = _fox_decode(page_table, q_bd.astype(BF16), g_rows, jnp.pad(kc3, pad_new).astype(BF16),
                         jnp.pad(vc3, pad_new).astype(BF16), cache_fox_k, cache_fox_v, lp8, l, ts)
        outs["lat_s"].append(lat.reshape(bs, ts, MLA_KVR))
        outs["pe_s"].append(kpe.reshape(bs, ts, MLA_ROPE))
        outs["fk_s"].append(kc3.reshape(bs, ts, FOX_H, FOX_HD))
        outs["fv_s"].append(vc3.reshape(bs, ts, FOX_H, FOX_HD))
        outs["lf_s"].append(fx[:, SM_F:SM_F + FOX_H].reshape(bs, ts, FOX_H))
        outs["st_s"].append(st)
        outs["cv_s"].append(y3[:, ts - (GDN_CONV - 1):, C_QKV:C_QKV + GDN_QKV])
        xs = mixer_tail(xs, mods_s, oa.reshape(bs * ts, GDN_H * GDN_DV), ob.reshape(bs * ts, MLA_H * MLA_V),
                        oc.reshape(bs * ts, FOX_H * FOX_HD), ts)

    stack = lambda k: jnp.stack(outs[k], axis=0)
    return (xp.reshape(bp, tp, d), xs.reshape(bs, ts, d),
            stack("lat_p"), stack("lat_s"), stack("pe_p"), stack("pe_s"),
            stack("fk_p"), stack("fk_s"), stack("fv_p"), stack("fv_s"),
            stack("lf_p"), stack("lf_s"), stack("st_p"), stack("st_s"),
            stack("cv_p"), stack("cv_s"))
```

```python
import functools
import math

import jax
import jax.numpy as jnp
from jax import lax
from jax.experimental import pallas as pl
from jax.experimental.pallas import tpu as pltpu

F32 = jnp.float32
BF16 = jnp.bfloat16
EPS = 1e-6
NEG = -1e30

GDN_H, GDN_DK, GDN_DV, GDN_CONV = 4, 128, 128, 4
GDN_QKV = GDN_H * (2 * GDN_DK + GDN_DV)
MLA_H, MLA_QR, MLA_KVR, MLA_NOPE, MLA_ROPE, MLA_V = 4, 256, 128, 64, 32, 64
FOX_H, FOX_HD = 4, 64
ROPE_BASE = 10000.0
MLA_SCALE = (MLA_NOPE + MLA_ROPE) ** -0.5
FOX_SCALE = FOX_HD ** -0.5

C_QKV, C_Z, C_CQ, C_QC, C_KC, C_VC, C_CKV, C_SM, N_Y = 0, 1536, 2048, 2304, 2560, 2816, 3072, 3200, 3328
SM_KPE, SM_KROT, SM_B, SM_A, SM_F = 0, 32, 64, 68, 72

V7X_VMEM_LIMIT = 52 * 1024 * 1024
ROW_TILE = 512


def _cparams(*sem):
    return pltpu.CompilerParams(dimension_semantics=sem, vmem_limit_bytes=V7X_VMEM_LIMIT)


def _dot(a, b):
    return jnp.dot(a, b, preferred_element_type=F32)


def _dot_nt(a, b):
    return lax.dot_general(a, b, (((1,), (1,)), ((), ())), preferred_element_type=F32)


def _split2(a):
    hi = a.astype(BF16)
    return hi, (a - hi.astype(F32)).astype(BF16)


def _split3(a):
    hi = a.astype(BF16)
    r = a - hi.astype(F32)
    mid = r.astype(BF16)
    return hi, mid, (r - mid.astype(F32)).astype(BF16)


def _dot_hi(a, b):
    ah, al = _split2(a)
    bh, bl = _split2(b)
    return _dot(ah, bh) + _dot(ah, bl) + _dot(al, bh)


def _dot_ones_lhs(ones_bf16, x):
    h, m, l = _split3(x)
    return _dot(ones_bf16, h) + _dot(ones_bf16, m) + _dot(ones_bf16, l)


def _dot_ones_rhs(x, ones_bf16):
    h, m, l = _split3(x)
    return _dot(h, ones_bf16) + _dot(m, ones_bf16) + _dot(l, ones_bf16)


def _sigmoid(x):
    return 1.0 / (1.0 + jnp.exp(-x))


def _silu(x):
    return x * _sigmoid(x)


def _softplus(x):
    return jnp.maximum(x, 0.0) + jnp.log(1.0 + jnp.exp(-jnp.abs(x)))


def _log_sigmoid(x):
    return -_softplus(-x)


def _rms(x):
    return x * lax.rsqrt(jnp.mean(x * x, axis=-1, keepdims=True) + EPS)


def _iota2(shape, dim):
    return lax.broadcasted_iota(jnp.int32, shape, dim)


def _ada_kernel(c_ref, w_ref, b_ref, o_ref):
    a = _silu(c_ref[...]).astype(BF16)
    o_ref[...] = _dot(a, w_ref[...].astype(BF16)) + b_ref[...]


def _adaln(c_all, w_ada, b_ada):
    n_layer, d, n = w_ada.shape
    r = c_all.shape[0]
    tn = 1536
    return pl.pallas_call(
        _ada_kernel,
        grid=(n_layer, n // tn),
        in_specs=[pl.BlockSpec((r, d), lambda l, j: (0, 0)),
                  pl.BlockSpec((None, d, tn), lambda l, j: (l, 0, j)),
                  pl.BlockSpec((None, 1, tn), lambda l, j: (l, 0, j))],
        out_specs=pl.BlockSpec((None, r, tn), lambda l, j: (l, 0, j)),
        out_shape=jax.ShapeDtypeStruct((n_layer, r, n), F32),
        compiler_params=_cparams("arbitrary", "arbitrary"),
        name="adaln",
    )(c_all, w_ada, b_ada.reshape(n_layer, 1, n))


def _mod_spec(tm, d, seq_len):
    if seq_len >= tm:
        return pl.BlockSpec((None, 1, d), lambda i, *_: ((i * tm) // seq_len, 0, 0))
    return pl.BlockSpec((tm, d), lambda i, *_: (i, 0))


def _in_proj_kernel(x_ref, g_ref, sc_ref, sh_ref, w_ref, o_ref):
    h = _rms(x_ref[...]) * g_ref[...]
    h = h * (1.0 + sc_ref[...]) + sh_ref[...]
    o_ref[...] = _dot(h.astype(BF16), w_ref[...])


def _in_proj(x, g, sc, sh, w, seq_len):
    r, d = x.shape
    n = w.shape[1]
    tm = 256
    ms = _mod_spec(tm, d, seq_len)
    return pl.pallas_call(
        _in_proj_kernel,
        grid=(r // tm,),
        in_specs=[pl.BlockSpec((tm, d), lambda i: (i, 0)),
                  pl.BlockSpec((1, d), lambda i: (0, 0)), ms, ms,
                  pl.BlockSpec((d, n), lambda i: (0, 0))],
        out_specs=pl.BlockSpec((tm, n), lambda i: (i, 0)),
        out_shape=jax.ShapeDtypeStruct((r, n), F32),
        compiler_params=_cparams("parallel"),
        name="in_proj",
    )(x, g, sc, sh, w)


def _gdn_prompt_kernel(qkv_ref, z_ref, sm_ref, cw_ref, alog_ref, dtb_ref, gn_ref, o_ref, so_ref,
                       ext_ref, s_ref, *, chunk, n_chunk):
    c = pl.program_id(1)
    n_h, dk = GDN_H, GDN_DK

    @pl.when(c == 0)
    def _():
        ext_ref[0:8, :] = jnp.zeros((8, GDN_QKV), F32)
        s_ref[...] = jnp.zeros_like(s_ref)

    ext_ref[8:8 + chunk, :] = qkv_ref[...]
    cw = cw_ref[...]
    conv = cw[0:1, :] * ext_ref[5:5 + chunk, :]
    for j in range(1, GDN_CONV):
        conv = conv + cw[j:j + 1, :] * ext_ref[5 + j:5 + j + chunk, :]
    ext_ref[0:8, :] = ext_ref[chunk:chunk + 8, :]
    act = _silu(conv)

    sm = sm_ref[...]
    g_log = -jnp.exp(alog_ref[...]) * _softplus(sm + dtb_ref[...])
    beta_all = _sigmoid(sm)
    row = _iota2((chunk, chunk), 0)
    col = _iota2((chunk, chunk), 1)
    incl = row >= col
    strict = row > col
    tri = jnp.where(incl, 1.0, 0.0).astype(BF16)
    eye = jnp.where(row == col, 1.0, 0.0)
    gc_all = _dot_ones_lhs(tri, g_log)
    gc_t = gc_all.T

    for h in range(n_h):
        q = act[:, h * dk:(h + 1) * dk]
        k = act[:, (n_h + h) * dk:(n_h + h + 1) * dk]
        v = act[:, 2 * n_h * dk + h * GDN_DV:2 * n_h * dk + (h + 1) * GDN_DV]
        q = q * lax.rsqrt(jnp.sum(q * q, axis=-1, keepdims=True) + EPS) * (dk ** -0.5)
        k = k * lax.rsqrt(jnp.sum(k * k, axis=-1, keepdims=True) + EPS)
        beta = beta_all[:, SM_B + h:SM_B + h + 1]
        gcol = gc_all[:, SM_A + h:SM_A + h + 1]
        grow = gc_t[SM_A + h:SM_A + h + 1, :]
        decay = jnp.where(incl, jnp.exp(jnp.where(incl, gcol - grow, 0.0)), 0.0)
        kb = k * beta
        vb = v * beta
        kbf = k.astype(BF16)
        x = -jnp.where(strict, _dot_nt(kb.astype(BF16), kbf) * decay, 0.0)
        minv = eye + x
        p = x
        for _ in range(int(math.log2(chunk)) - 1):
            p = _dot_hi(p, p)
            minv = minv + _dot_hi(minv, p)
        u = _dot_hi(minv, vb)
        w = _dot_hi(minv, kb * jnp.exp(gcol))
        aqk = _dot_nt(q.astype(BF16), kbf) * decay
        s_h = s_ref[h]
        s_bf = s_h.astype(BF16)
        v_new = u - _dot(w.astype(BF16), s_bf)
        o = _dot((q * jnp.exp(gcol)).astype(BF16), s_bf) + _dot(aqk.astype(BF16), v_new.astype(BF16))
        g_last = gcol[chunk - 1:chunk, :]
        k_dec_t = (k * jnp.exp(g_last - gcol)).T
        s_ref[h] = s_h * jnp.exp(g_last) + _dot(k_dec_t.astype(BF16), v_new.astype(BF16))
        o = _rms(o) * gn_ref[...] * _silu(z_ref[:, h * GDN_DV:(h + 1) * GDN_DV])
        o_ref[:, h * GDN_DV:(h + 1) * GDN_DV] = o.astype(o_ref.dtype)

    @pl.when(c == n_chunk - 1)
    def _():
        so_ref[...] = s_ref[...]


def _gdn_prompt(y, cw, alog_row, dtb_row, gn, batch, seq_len):
    chunk = 128
    n_chunk = seq_len // chunk
    r = y.shape[0]
    kern = functools.partial(_gdn_prompt_kernel, chunk=chunk, n_chunk=n_chunk)
    return pl.pallas_call(
        kern,
        grid=(batch, n_chunk),
        in_specs=[pl.BlockSpec((chunk, GDN_QKV), lambda b, c: (b * n_chunk + c, C_QKV // GDN_QKV)),
                  pl.BlockSpec((chunk, 512), lambda b, c: (b * n_chunk + c, C_Z // 512)),
                  pl.BlockSpec((chunk, 128), lambda b, c: (b * n_chunk + c, C_SM // 128)),
                  pl.BlockSpec((GDN_CONV, GDN_QKV), lambda b, c: (0, 0)),
                  pl.BlockSpec((1, 128), lambda b, c: (0, 0)),
                  pl.BlockSpec((1, 128), lambda b, c: (0, 0)),
                  pl.BlockSpec((1, GDN_DV), lambda b, c: (0, 0))],
        out_specs=[pl.BlockSpec((chunk, GDN_H * GDN_DV), lambda b, c: (b * n_chunk + c, 0)),
                   pl.BlockSpec((None, GDN_H, GDN_DK, GDN_DV), lambda b, c: (b, 0, 0, 0))],
        out_shape=[jax.ShapeDtypeStruct((r, GDN_H * GDN_DV), BF16),
                   jax.ShapeDtypeStruct((batch, GDN_H, GDN_DK, GDN_DV), F32)],
        scratch_shapes=[pltpu.VMEM((chunk + 8, GDN_QKV), F32),
                        pltpu.VMEM((GDN_H, GDN_DK, GDN_DV), F32)],
        compiler_params=_cparams("arbitrary", "arbitrary"),
        name="gdn_prompt",
    )(y, y, y, cw, alog_row, dtb_row, gn)


def _gdn_decode_kernel(qkv_ref, z_ref, sm_ref, buf_ref, s0_ref, cw_ref, alog_ref, dtb_ref, gn_ref,
                       o_ref, so_ref, ext_ref, qk_ref, *, n_tok):
    n_h, dk = GDN_H, GDN_DK
    ext_ref[0:GDN_CONV - 1, :] = buf_ref[...]
    ext_ref[GDN_CONV - 1:GDN_CONV - 1 + n_tok, :] = qkv_ref[...]
    cw = cw_ref[...]
    conv = cw[0:1, :] * ext_ref[0:n_tok, :]
    for j in range(1, GDN_CONV):
        conv = conv + cw[j:j + 1, :] * ext_ref[j:j + n_tok, :]
    act = _silu(conv)
    sm = sm_ref[...]
    decay_all = jnp.exp(-jnp.exp(alog_ref[...]) * _softplus(sm + dtb_ref[...]))
    beta_all = _sigmoid(sm)

    qk_ref[...] = jnp.zeros_like(qk_ref)
    for h in range(n_h):
        q = act[:, h * dk:(h + 1) * dk]
        k = act[:, (n_h + h) * dk:(n_h + h + 1) * dk]
        q = q * lax.rsqrt(jnp.sum(q * q, axis=-1, keepdims=True) + EPS) * (dk ** -0.5)
        k = k * lax.rsqrt(jnp.sum(k * k, axis=-1, keepdims=True) + EPS)
        qk_ref[h * n_tok:(h + 1) * n_tok, :] = q
        qk_ref[(n_h + h) * n_tok:(n_h + h + 1) * n_tok, :] = k
    qk_t = qk_ref[...].T

    for h in range(n_h):
        s = s0_ref[h]
        for t in range(n_tok):
            q_col = qk_t[:, h * n_tok + t:h * n_tok + t + 1]
            k_col = qk_t[:, (n_h + h) * n_tok + t:(n_h + h) * n_tok + t + 1]
            v_row = act[t:t + 1, 2 * n_h * dk + h * GDN_DV:2 * n_h * dk + (h + 1) * GDN_DV]
            s = s * decay_all[t:t + 1, SM_A + h:SM_A + h + 1]
            pred = jnp.sum(k_col * s, axis=0, keepdims=True)
            v_new = beta_all[t:t + 1, SM_B + h:SM_B + h + 1] * (v_row - pred)
            s = s + k_col * v_new
            o = jnp.sum(q_col * s, axis=0, keepdims=True)
            o = _rms(o) * gn_ref[...] * _silu(z_ref[t:t + 1, h * GDN_DV:(h + 1) * GDN_DV])
            o_ref[t:t + 1, h * GDN_DV:(h + 1) * GDN_DV] = o.astype(o_ref.dtype)
        so_ref[h] = s


def _gdn_decode(y3, buf, s0, cw, alog_row, dtb_row, gn):
    batch, n_tok, _ = y3.shape
    kern = functools.partial(_gdn_decode_kernel, n_tok=n_tok)
    return pl.pallas_call(
        kern,
        grid=(batch,),
        in_specs=[pl.BlockSpec((None, n_tok, GDN_QKV), lambda b: (b, 0, C_QKV // GDN_QKV)),
                  pl.BlockSpec((None, n_tok, 512), lambda b: (b, 0, C_Z // 512)),
                  pl.BlockSpec((None, n_tok, 128), lambda b: (b, 0, C_SM // 128)),
                  pl.BlockSpec((None, GDN_CONV - 1, GDN_QKV), lambda b: (b, 0, 0)),
                  pl.BlockSpec((None, GDN_H, GDN_DK, GDN_DV), lambda b: (b, 0, 0, 0)),
                  pl.BlockSpec((GDN_CONV, GDN_QKV), lambda b: (0, 0)),
                  pl.BlockSpec((1, 128), lambda b: (0, 0)),
                  pl.BlockSpec((1, 128), lambda b: (0, 0)),
                  pl.BlockSpec((1, GDN_DV), lambda b: (0, 0))],
        out_specs=[pl.BlockSpec((None, n_tok, GDN_H * GDN_DV), lambda b: (b, 0, 0)),
                   pl.BlockSpec((None, GDN_H, GDN_DK, GDN_DV), lambda b: (b, 0, 0, 0))],
        out_shape=[jax.ShapeDtypeStruct((batch, n_tok, GDN_H * GDN_DV), F32),
                   jax.ShapeDtypeStruct((batch, GDN_H, GDN_DK, GDN_DV), F32)],
        scratch_shapes=[pltpu.VMEM((16, GDN_QKV), F32), pltpu.VMEM((128, GDN_DK), F32)],
        compiler_params=_cparams("arbitrary"),
        name="gdn_decode",
    )(y3, y3, y3, buf, s0, cw, alog_row, dtb_row, gn)


def _mla_prep_kernel(cq_ref, ckv_ref, sm_ref, cs_ref, gq_ref, gkv_ref, wq_ref, wkb_ref,
                     qcat_ref, kcat_ref, lat_ref, kpe_ref):
    tm = cq_ref.shape[0]
    cs = cs_ref[...]
    cos = cs[:, 0:MLA_ROPE]
    sin = cs[:, MLA_ROPE:2 * MLA_ROPE]
    lat = _rms(ckv_ref[...]) * gkv_ref[...]
    sm = sm_ref[...]
    kpe = sm[:, SM_KPE:SM_KPE + MLA_ROPE] * cos + sm[:, SM_KROT:SM_KROT + MLA_ROPE] * sin
    lat_ref[...] = lat
    kpe_ref[...] = kpe
    pad = jnp.zeros((tm, 256 - MLA_KVR - MLA_ROPE), F32)
    kcat_ref[:, 0:MLA_KVR] = lat.astype(kcat_ref.dtype)
    kcat_ref[:, MLA_KVR:MLA_KVR + MLA_ROPE] = kpe.astype(kcat_ref.dtype)
    kcat_ref[:, MLA_KVR + MLA_ROPE:256] = pad.astype(kcat_ref.dtype)

    qn = _rms(cq_ref[...]) * gq_ref[...]
    qb = _dot(qn.astype(BF16), wq_ref[...])
    for h in range(MLA_H):
        base = h * 128
        nope = qb[:, base:base + MLA_NOPE]
        pe = (qb[:, base + MLA_NOPE:base + MLA_NOPE + MLA_ROPE] * cos
              + qb[:, base + MLA_NOPE + MLA_ROPE:base + 128] * sin)
        q_abs = _dot(nope.astype(BF16), wkb_ref[h])
        qcat_ref[h, :, 0:MLA_KVR] = (q_abs * MLA_SCALE).astype(qcat_ref.dtype)
        qcat_ref[h, :, MLA_KVR:MLA_KVR + MLA_ROPE] = (pe * MLA_SCALE).astype(qcat_ref.dtype)
        qcat_ref[h, :, MLA_KVR + MLA_ROPE:256] = pad.astype(qcat_ref.dtype)


def _mla_prep(y, cs, gq, gkv, wq, wkb):
    r = y.shape[0]
    tm = ROW_TILE
    n_tab = cs.shape[0] // tm
    return pl.pallas_call(
        _mla_prep_kernel,
        grid=(r // tm,),
        in_specs=[pl.BlockSpec((tm, MLA_QR), lambda i: (i, C_CQ // MLA_QR)),
                  pl.BlockSpec((tm, MLA_KVR), lambda i: (i, C_CKV // MLA_KVR)),
                  pl.BlockSpec((tm, 128), lambda i: (i, C_SM // 128)),
                  pl.BlockSpec((tm, 2 * MLA_ROPE), lambda i: (i % n_tab, 0)),
                  pl.BlockSpec((1, MLA_QR), lambda i: (0, 0)),
                  pl.BlockSpec((1, MLA_KVR), lambda i: (0, 0)),
                  pl.BlockSpec((MLA_QR, MLA_H * 128), lambda i: (0, 0)),
                  pl.BlockSpec((MLA_H, MLA_NOPE, MLA_KVR), lambda i: (0, 0, 0))],
        out_specs=[pl.BlockSpec((MLA_H, tm, 256), lambda i: (0, i, 0)),
                   pl.BlockSpec((tm, 256), lambda i: (i, 0)),
                   pl.BlockSpec((tm, MLA_KVR), lambda i: (i, 0)),
                   pl.BlockSpec((tm, MLA_ROPE), lambda i: (i, 0))],
        out_shape=[jax.ShapeDtypeStruct((MLA_H, r, 256), BF16),
                   jax.ShapeDtypeStruct((r, 256), BF16),
                   jax.ShapeDtypeStruct((r, MLA_KVR), F32),
                   jax.ShapeDtypeStruct((r, MLA_ROPE), F32)],
        compiler_params=_cparams("parallel"),
        name="mla_prep",
    )(y, y, y, cs, gq, gkv, wq, wkb)


def _fox_prep_kernel(*refs, seq_len, tm, emit_heads):
    if emit_heads:
        qc_ref, kc_ref, vc_ref, sm_ref, fb_ref, q_ref, k_ref, v_ref, fx_ref, carry_ref = refs
    else:
        sm_ref, fb_ref, fx_ref, carry_ref = refs
    i = pl.program_id(0)
    logf = _log_sigmoid(sm_ref[...] + fb_ref[...])
    row = _iota2((tm, tm), 0)
    col = _iota2((tm, tm), 1)
    if seq_len >= tm:
        @pl.when((i * tm) % seq_len == 0)
        def _():
            carry_ref[...] = jnp.zeros_like(carry_ref)
        tri = jnp.where(row >= col, 1.0, 0.0).astype(BF16)
        g = _dot_ones_lhs(tri, logf) + carry_ref[...]
        carry_ref[...] = g[tm - 1:tm, :]
    else:
        same = (row // seq_len) == (col // seq_len)
        tri = jnp.where((row >= col) & same, 1.0, 0.0).astype(BF16)
        g = _dot_ones_lhs(tri, logf)
    fx_ref[:, 0:128] = logf
    fx_ref[:, 128:256] = g
    if emit_heads:
        for h in range(FOX_H):
            sl = slice(h * FOX_HD, (h + 1) * FOX_HD)
            q_ref[h] = (qc_ref[:, sl] * FOX_SCALE).astype(BF16)
            k_ref[h] = kc_ref[:, sl].astype(BF16)
            v_ref[h] = vc_ref[:, sl].astype(BF16)


def _fox_prep(y, fb_row, seq_len, emit_heads):
    r = y.shape[0]
    tm = ROW_TILE
    kern = functools.partial(_fox_prep_kernel, seq_len=seq_len, tm=tm, emit_heads=emit_heads)
    sm_spec = pl.BlockSpec((tm, 128), lambda i: (i, C_SM // 128))
    fb_spec = pl.BlockSpec((1, 128), lambda i: (0, 0))
    fx_spec = pl.BlockSpec((tm, 256), lambda i: (i, 0))
    fx_shape = jax.ShapeDtypeStruct((r, 256), F32)
    w = FOX_H * FOX_HD
    if emit_heads:
        head_spec = pl.BlockSpec((FOX_H, tm, FOX_HD), lambda i: (0, i, 0))
        head_shape = jax.ShapeDtypeStruct((FOX_H, r, FOX_HD), BF16)
        in_specs = [pl.BlockSpec((tm, w), lambda i: (i, C_QC // w)),
                    pl.BlockSpec((tm, w), lambda i: (i, C_KC // w)),
                    pl.BlockSpec((tm, w), lambda i: (i, C_VC // w)), sm_spec, fb_spec]
        args = (y, y, y, y, fb_row)
        out_specs = [head_spec, head_spec, head_spec, fx_spec]
        out_shape = [head_shape, head_shape, head_shape, fx_shape]
    else:
        in_specs = [sm_spec, fb_spec]
        args = (y, fb_row)
        out_specs = [fx_spec]
        out_shape = [fx_shape]
    return pl.pallas_call(
        kern, grid=(r // tm,), in_specs=in_specs, out_specs=out_specs, out_shape=out_shape,
        scratch_shapes=[pltpu.VMEM((1, 128), F32)],
        compiler_params=_cparams("arbitrary"),
        name="fox_prep",
    )(*args)


def _flash_kernel(*refs, n_head, shared_kv, has_bias, has_proj, d_v, d_out):
    refs = list(refs)
    q_ref, k_ref, v_ref = refs[:3]
    pos = 3
    if has_bias:
        gq_ref, gk_ref = refs[pos:pos + 2]
        pos += 2
    if has_proj:
        wp_ref = refs[pos]
        pos += 1
    o_ref, m_ref, l_ref, acc_ref = refs[pos:pos + 4]
    qi = pl.program_id(1)
    ki = pl.program_id(2)
    tq = q_ref.shape[1]
    tk = k_ref.shape[-2]

    @pl.when(ki == 0)
    def _():
        m_ref[...] = jnp.full(m_ref.shape, NEG, F32)
        l_ref[...] = jnp.zeros_like(l_ref)
        acc_ref[...] = jnp.zeros_like(acc_ref)

    def update(masked):
        for h in range(n_head):
            k = k_ref[...] if shared_kv else k_ref[h]
            v = v_ref[...] if shared_kv else v_ref[h]
            s = _dot_nt(q_ref[h], k)
            if has_bias:
                s = s + gq_ref[h] - gk_ref[h]
            if masked:
                s = jnp.where(_iota2((tq, tk), 1) <= _iota2((tq, tk), 0), s, NEG)
            m_prev = m_ref[h]
            m_new = jnp.maximum(m_prev, jnp.max(s, axis=-1, keepdims=True))
            alpha = jnp.exp(m_prev - m_new)
            p = jnp.exp(s - m_new)
            l_ref[h] = alpha * l_ref[h] + jnp.sum(p, axis=-1, keepdims=True)
            acc_ref[h] = alpha * acc_ref[h] + _dot(p.astype(BF16), v)
            m_ref[h] = m_new

    @pl.when(ki < qi)
    def _():
        update(False)

    @pl.when(ki == qi)
    def _():
        update(True)
        for h in range(n_head):
            o = acc_ref[h] / l_ref[h]
            if has_proj:
                o = _dot(o.astype(BF16), wp_ref[h])
            o_ref[:, h * d_out:(h + 1) * d_out] = o.astype(o_ref.dtype)


def _flash(q, k, v, batch, seq_len, *, shared_kv, gq=None, gk=None, wp=None, d_v, d_out):
    n_head, r, d_k = q.shape
    t = ROW_TILE
    nb = seq_len // t
    has_bias = gq is not None
    has_proj = wp is not None
    kern = functools.partial(_flash_kernel, n_head=n_head, shared_kv=shared_kv, has_bias=has_bias,
                             has_proj=has_proj, d_v=d_v, d_out=d_out)
    q_spec = pl.BlockSpec((n_head, t, d_k), lambda b, i, j: (0, b * nb + i, 0))
    if shared_kv:
        k_spec = pl.BlockSpec((t, d_k), lambda b, i, j: (b * nb + jnp.minimum(i, j), 0))
        v_spec = pl.BlockSpec((t, d_v), lambda b, i, j: (b * nb + jnp.minimum(i, j), 0))
    else:
        k_spec = pl.BlockSpec((n_head, t, d_k), lambda b, i, j: (0, b * nb + jnp.minimum(i, j), 0))
        v_spec = pl.BlockSpec((n_head, t, d_v), lambda b, i, j: (0, b * nb + jnp.minimum(i, j), 0))
    in_specs = [q_spec, k_spec, v_spec]
    args = [q, k, v]
    if has_bias:
        in_specs += [pl.BlockSpec((n_head, t, 1), lambda b, i, j: (0, b * nb + i, 0)),
                     pl.BlockSpec((n_head, 1, t), lambda b, i, j: (0, 0, b * nb + jnp.minimum(i, j)))]
        args += [gq, gk]
    if has_proj:
        in_specs.append(pl.BlockSpec(wp.shape, lambda b, i, j: (0, 0, 0)))
        args.append(wp)
    return pl.pallas_call(
        kern,
        grid=(batch, nb, nb),
        in_specs=in_specs,
        out_specs=pl.BlockSpec((t, n_head * d_out), lambda b, i, j: (b * nb + i, 0)),
        out_shape=jax.ShapeDtypeStruct((r, n_head * d_out), BF16),
        scratch_shapes=[pltpu.VMEM((n_head, t, 1), F32), pltpu.VMEM((n_head, t, 1), F32),
                        pltpu.VMEM((n_head, t, d_v), F32)],
        compiler_params=_cparams("parallel", "arbitrary", "arbitrary"),
        name="flash_shared" if shared_kv else "flash_heads",
    )(*args)


PAGES_PER_STEP = 16


def _softmax_step(s, m_ref, l_ref):
    m_prev = m_ref[...]
    m_new = jnp.maximum(m_prev, jnp.max(s, axis=-1, keepdims=True))
    alpha = jnp.exp(m_prev - m_new)
    p = jnp.exp(s - m_new)
    l_ref[...] = alpha * l_ref[...] + jnp.sum(p, axis=-1, keepdims=True)
    m_ref[...] = m_new
    return alpha, p


def _new_token_mask(n_rows, n_keys, n_tok, n_head):
    t = _iota2((n_rows, n_keys), 0) // n_head
    j = _iota2((n_rows, n_keys), 1)
    return (j <= t) & (j < n_tok)


def _mla_decode_kernel(pt_ref, q_ref, knew_ref, wvb_ref, *refs, pps, n_step, n_tok):
    lat_refs = refs[:pps]
    pe_refs = refs[pps:2 * pps]
    o_ref, m_ref, l_ref, acc_ref = refs[2 * pps:2 * pps + 4]
    c = pl.program_id(1)
    n_rows = q_ref.shape[0]

    @pl.when(c == 0)
    def _():
        m_ref[...] = jnp.full(m_ref.shape, NEG, F32)
        l_ref[...] = jnp.zeros_like(l_ref)
        acc_ref[...] = jnp.zeros_like(acc_ref)

    q = q_ref[...]
    q_abs = q[:, 0:MLA_KVR]
    q_pe = q[:, MLA_KVR:MLA_KVR + MLA_ROPE]
    lat = jnp.concatenate([lat_refs[j][...].astype(BF16) for j in range(pps)], axis=0)
    pe_t = jnp.concatenate([pe_refs[j][...].astype(BF16) for j in range(pps)], axis=1)
    s = _dot_nt(q_abs, lat) + _dot(q_pe, pe_t)
    alpha, p = _softmax_step(s, m_ref, l_ref)
    acc_ref[...] = alpha * acc_ref[...] + _dot(p.astype(BF16), lat)

    @pl.when(c == n_step - 1)
    def _():
        kn = knew_ref[...]
        s_new = _dot_nt(q, kn)
        s_new = jnp.where(_new_token_mask(n_rows, kn.shape[0], n_tok, MLA_H), s_new, NEG)
        alpha2, p2 = _softmax_step(s_new, m_ref, l_ref)
        o = (alpha2 * acc_ref[...] + _dot(p2.astype(BF16), kn[:, 0:MLA_KVR])) / l_ref[...]
        full = _dot(o.astype(BF16), wvb_ref[...])
        head = _iota2((n_rows, MLA_V), 0) % MLA_H
        out = jnp.zeros((n_rows, MLA_V), F32)
        for h in range(MLA_H):
            out = out + jnp.where(head == h, full[:, h * MLA_V:(h + 1) * MLA_V], 0.0)
        o_ref[...] = out.astype(o_ref.dtype)


def _mla_decode(page_table, q, knew, wvb, cache_lat, cache_pe_t, layer, n_tok):
    batch, n_rows, _ = q.shape
    n_pages = page_table.shape[1]
    pps = PAGES_PER_STEP
    n_step = n_pages // pps
    kern = functools.partial(_mla_decode_kernel, pps=pps, n_step=n_step, n_tok=n_tok)

    def page_spec(arr, j):
        return pl.BlockSpec((None, None) + arr.shape[2:], lambda b, c, pt: (layer, pt[b, c * pps + j], 0, 0))

    in_specs = ([pl.BlockSpec((None, n_rows, 256), lambda b, c, pt: (b, 0, 0)),
                 pl.BlockSpec((None, knew.shape[1], 256), lambda b, c, pt: (b, 0, 0)),
                 pl.BlockSpec(wvb.shape, lambda b, c, pt: (0, 0))]
                + [page_spec(cache_lat, j) for j in range(pps)]
                + [page_spec(cache_pe_t, j) for j in range(pps)])
    return pl.pallas_call(
        kern,
        grid_spec=pltpu.PrefetchScalarGridSpec(
            num_scalar_prefetch=1, grid=(batch, n_step), in_specs=in_specs,
            out_specs=pl.BlockSpec((None, n_rows, MLA_V), lambda b, c, pt: (b, 0, 0)),
            scratch_shapes=[pltpu.VMEM((n_rows, 1), F32), pltpu.VMEM((n_rows, 1), F32),
                            pltpu.VMEM((n_rows, MLA_KVR), F32)]),
        out_shape=jax.ShapeDtypeStruct((batch, n_rows, MLA_V), BF16),
        compiler_params=_cparams("arbitrary", "arbitrary"),
        name="mla_decode",
    )(page_table, q, knew, wvb, *([cache_lat] * pps), *([cache_pe_t] * pps))


def _fox_decode_kernel(pt_ref, q_ref, g_ref, knew_ref, vnew_ref, *refs, pps, n_step, n_tok):
    k_refs = refs[:pps]
    v_refs = refs[pps:2 * pps]
    lp_refs = refs[2 * pps:3 * pps]
    o_ref, m_ref, l_ref, acc_ref, tot_ref = refs[3 * pps:3 * pps + 5]
    c = pl.program_id(1)
    n_rows = q_ref.shape[0]
    n_h, hd = FOX_H, FOX_HD
    page = lp_refs[0].shape[-1]

    @pl.when(c == 0)
    def _():
        m_ref[...] = jnp.full(m_ref.shape, NEG, F32)
        l_ref[...] = jnp.zeros_like(l_ref)
        acc_ref[...] = jnp.zeros_like(acc_ref)
        tot_ref[...] = jnp.zeros_like(tot_ref)

    q = q_ref[...]
    later = jnp.where(_iota2((page, page), 0) >= _iota2((page, page), 1), 1.0, 0.0).astype(BF16)
    k_t = jnp.concatenate([k_refs[j][...].astype(BF16) for j in range(pps)], axis=1)
    v_t = jnp.concatenate([v_refs[j][...].astype(BF16) for j in range(pps)], axis=1)
    lp_all = jnp.concatenate([lp_refs[j][...] for j in range(pps)], axis=0)
    rev_all = _dot_ones_rhs(lp_all, later)
    tot = tot_ref[...]
    parts = []
    for j in range(pps):
        rev = rev_all[j * 2 * n_h:(j + 1) * 2 * n_h, :]
        parts.append(tot + rev - lp_all[j * 2 * n_h:(j + 1) * 2 * n_h, :])
        tot = tot + rev[:, 0:1]
    tot_ref[...] = tot
    suffix = jnp.concatenate(parts, axis=1)
    s = (_dot(q, k_t) * FOX_SCALE + jnp.concatenate([suffix] * (n_rows // (2 * n_h)), axis=0) + g_ref[...])
    alpha, p = _softmax_step(s, m_ref, l_ref)
    acc_ref[...] = alpha * acc_ref[...] + _dot_nt(p.astype(BF16), v_t)

    @pl.when(c == n_step - 1)
    def _():
        kn = knew_ref[...]
        gcol = g_ref[...]
        s_new = _dot_nt(q, kn) * FOX_SCALE
        rows_i = _iota2((n_rows, kn.shape[0]), 0)
        cols_j = _iota2((n_rows, kn.shape[0]), 1)
        gk = jnp.zeros((n_rows, kn.shape[0]), F32)
        for j in range(n_tok):
            for h in range(n_h):
                gj = gcol[j * n_h + h:j * n_h + h + 1, :]
                gk = gk + jnp.where((cols_j == j) & (rows_i % n_h == h), gj, 0.0)
        s_new = s_new + gcol - gk
        s_new = jnp.where(_new_token_mask(n_rows, kn.shape[0], n_tok, n_h), s_new, NEG)
        alpha2, p2 = _softmax_step(s_new, m_ref, l_ref)
        full = alpha2 * acc_ref[...] + _dot(p2.astype(BF16), vnew_ref[...])
        head = _iota2((n_rows, hd), 0) % n_h
        out = jnp.zeros((n_rows, hd), F32)
        for h in range(n_h):
            out = out + jnp.where(head == h, full[:, h * hd:(h + 1) * hd], 0.0)
        o_ref[...] = (out / l_ref[...]).astype(o_ref.dtype)


def _fox_decode(page_table, q, gcol, knew, vnew, cache_kt, cache_vt, lp8, layer, n_tok):
    batch, n_rows, _ = q.shape
    n_pages = page_table.shape[1]
    pps = PAGES_PER_STEP
    n_step = n_pages // pps
    page = cache_kt.shape[3]
    kern = functools.partial(_fox_decode_kernel, pps=pps, n_step=n_step, n_tok=n_tok)

    def kv_spec(j):
        return pl.BlockSpec((None, None, FOX_H * FOX_HD, page),
                            lambda b, c, pt: (layer, pt[b, n_pages - 1 - (c * pps + j)], 0, 0))

    def lp_spec(j):
        return pl.BlockSpec((None, None, 2 * FOX_H, page),
                            lambda b, c, pt: (layer, pt[b, n_pages - 1 - (c * pps + j)], 0, 0))

    w = FOX_H * FOX_HD
    in_specs = ([pl.BlockSpec((None, n_rows, w), lambda b, c, pt: (b, 0, 0)),
                 pl.BlockSpec((None, n_rows, 1), lambda b, c, pt: (b, 0, 0)),
                 pl.BlockSpec((None, knew.shape[1], w), lambda b, c, pt: (b, 0, 0)),
                 pl.BlockSpec((None, vnew.shape[1], w), lambda b, c, pt: (b, 0, 0))]
                + [kv_spec(j) for j in range(pps)] + [kv_spec(j) for j in range(pps)]
                + [lp_spec(j) for j in range(pps)])
    return pl.pallas_call(
        kern,
        grid_spec=pltpu.PrefetchScalarGridSpec(
            num_scalar_prefetch=1, grid=(batch, n_step), in_specs=in_specs,
            out_specs=pl.BlockSpec((None, n_rows, FOX_HD), lambda b, c, pt: (b, 0, 0)),
            scratch_shapes=[pltpu.VMEM((n_rows, 1), F32), pltpu.VMEM((n_rows, 1), F32),
                            pltpu.VMEM((n_rows, FOX_H * FOX_HD), F32), pltpu.VMEM((2 * FOX_H, page), F32)]),
        out_shape=jax.ShapeDtypeStruct((batch, n_rows, FOX_HD), BF16),
        compiler_params=_cparams("arbitrary", "arbitrary"),
        name="fox_decode",
    )(page_table, q, gcol, knew, vnew, *([cache_kt] * pps), *([cache_vt] * pps), *([lp8] * pps))


def _out_proj_kernel(oa_ref, ob_ref, oc_ref, w_ref, x_ref, gt_ref, o_ref):
    na, nb = oa_ref.shape[1], ob_ref.shape[1]
    acc = _dot(oa_ref[...].astype(BF16), w_ref[0:na, :])
    acc = acc + _dot(ob_ref[...].astype(BF16), w_ref[na:na + nb, :])
    acc = acc + _dot(oc_ref[...].astype(BF16), w_ref[na + nb:, :])
    o_ref[...] = x_ref[...] + (1.0 + gt_ref[...]) * acc


def _out_proj(oa, ob, oc, w, x, gt, seq_len):
    r, d = x.shape
    tm = ROW_TILE
    return pl.pallas_call(
        _out_proj_kernel,
        grid=(r // tm,),
        in_specs=[pl.BlockSpec((tm, oa.shape[1]), lambda i: (i, 0)),
                  pl.BlockSpec((tm, ob.shape[1]), lambda i: (i, 0)),
                  pl.BlockSpec((tm, oc.shape[1]), lambda i: (i, 0)),
                  pl.BlockSpec(w.shape, lambda i: (0, 0)),
                  pl.BlockSpec((tm, d), lambda i: (i, 0)),
                  _mod_spec(tm, d, seq_len)],
        out_specs=pl.BlockSpec((tm, d), lambda i: (i, 0)),
        out_shape=jax.ShapeDtypeStruct((r, d), F32),
        compiler_params=_cparams("parallel"),
        name="out_proj",
    )(oa, ob, oc, w, x, gt)


def _ffn_kernel(*refs, n_expert, routed, final_norm):
    refs = list(refs)
    x_ref, g_ref, sc_ref, sh_ref, gt_ref = refs[:5]
    pos = 5
    if routed:
        rt_ref = refs[pos]
        pos += 1
    wg_ref, wu_ref, wd_ref = refs[pos:pos + 3]
    pos += 3
    if final_norm:
        gf_ref = refs[pos]
        pos += 1
    o_ref, h_ref, acc_ref = refs[pos:pos + 3]
    if routed:
        gates_ref = refs[pos + 3]
    e = pl.program_id(1)
    f = pl.program_id(2)
    tm = x_ref.shape[0]

    @pl.when((e == 0) & (f == 0))
    def _():
        h = _rms(x_ref[...]) * g_ref[...]
        h = h * (1.0 + sc_ref[...]) + sh_ref[...]
        h_ref[...] = h.astype(BF16)
        acc_ref[...] = jnp.zeros_like(acc_ref)
        if routed:
            lane = _iota2((tm, 128), 1)
            logits = jnp.where(lane < n_expert, _dot_hi(h, rt_ref[...]), NEG)
            m1 = jnp.max(logits, axis=-1, keepdims=True)
            i1 = jnp.min(jnp.where(logits == m1, lane, 128), axis=-1, keepdims=True)
            rest = jnp.where(lane == i1, NEG, logits)
            m2 = jnp.max(rest, axis=-1, keepdims=True)
            i2 = jnp.min(jnp.where(rest == m2, lane, 128), axis=-1, keepdims=True)
            e2 = jnp.exp(m2 - m1)
            w1 = 1.0 / (1.0 + e2)
            gates_ref[...] = jnp.where(lane == i1, w1, 0.0) + jnp.where(lane == i2, e2 * w1, 0.0)

    hb = h_ref[...]
    a = _dot(hb, wg_ref[...])
    u = _dot(hb, wu_ref[...])
    y = _dot((_silu(a) * u).astype(BF16), wd_ref[...])
    if routed:
        lane = _iota2((tm, 128), 1)
        y = y * jnp.sum(jnp.where(lane == e, gates_ref[...], 0.0), axis=-1, keepdims=True)
    acc_ref[...] += y

    @pl.when((e == n_expert - 1) & (f == pl.num_programs(2) - 1))
    def _():
        out = x_ref[...] + (1.0 + gt_ref[...]) * acc_ref[...]
        if final_norm:
            out = _rms(out) * gf_ref[...]
        o_ref[...] = out


def _ffn(x, g, sc, sh, gt, wg, wu, wd, seq_len, router=None, g_final=None):
    r, d = x.shape
    n_expert, _, d_ff = wg.shape
    tm = ROW_TILE
    tf = d_ff // 2
    routed = router is not None
    final_norm = g_final is not None
    ms = _mod_spec(tm, d, seq_len)
    in_specs = [pl.BlockSpec((tm, d), lambda i, e, f: (i, 0)),
                pl.BlockSpec((1, d), lambda i, e, f: (0, 0)), ms, ms, ms]
    args = [x, g, sc, sh, gt]
    if routed:
        in_specs.append(pl.BlockSpec(router.shape, lambda i, e, f: (0, 0)))
        args.append(router)
    in_specs += [pl.BlockSpec((None, d, tf), lambda i, e, f: (e, 0, f)),
                 pl.BlockSpec((None, d, tf), lambda i, e, f: (e, 0, f)),
                 pl.BlockSpec((None, tf, d), lambda i, e, f: (e, f, 0))]
    args += [wg, wu, wd]
    if final_norm:
        in_specs.append(pl.BlockSpec((1, d), lambda i, e, f: (0, 0)))
        args.append(g_final)
    scratch = [pltpu.VMEM((tm, d), BF16), pltpu.VMEM((tm, d), F32)]
    if routed:
        scratch.append(pltpu.VMEM((tm, 128), F32))
    kern = functools.partial(_ffn_kernel, n_expert=n_expert, routed=routed, final_norm=final_norm)
    return pl.pallas_call(
        kern,
        grid=(r // tm, n_expert, d_ff // tf),
        in_specs=in_specs,
        out_specs=pl.BlockSpec((tm, d), lambda i, e, f: (i, 0)),
        out_shape=jax.ShapeDtypeStruct((r, d), F32),
        scratch_shapes=scratch,
        compiler_params=_cparams("parallel", "arbitrary", "arbitrary"),
        name="moe_ffn" if routed else "dense_ffn",
    )(*args)


def _rot_cols(w):
    half = w.shape[-1] // 2
    return jnp.concatenate([-w[..., half:], w[..., :half]], axis=-1)


def _prep_w_in(w):
    d = w.shape[0]
    sizes = (GDN_QKV, GDN_H * GDN_DV, GDN_H, GDN_H, MLA_QR, MLA_KVR, MLA_ROPE,
             FOX_H * FOX_HD, FOX_H * FOX_HD, FOX_H * FOX_HD, FOX_H)
    pieces, off = [], 0
    for s in sizes:
        pieces.append(w[:, off:off + s])
        off += s
    qkv, z, b, a, cq, ckv, kpe, qc, kc, vc, f = pieces
    used = 2 * MLA_ROPE + 2 * GDN_H + FOX_H
    small = jnp.concatenate([kpe, _rot_cols(kpe), b, a, f, jnp.zeros((d, 128 - used), w.dtype)], axis=1)
    return jnp.concatenate([qkv, z, cq, qc, kc, vc, ckv, small], axis=1).astype(BF16)


def _prep_w_qb(w):
    per = MLA_NOPE + MLA_ROPE
    cols = []
    for h in range(MLA_H):
        nope = w[:, h * per:h * per + MLA_NOPE]
        pe = w[:, h * per + MLA_NOPE:(h + 1) * per]
        cols += [nope, pe, _rot_cols(pe)]
    return jnp.concatenate(cols, axis=1).astype(BF16)


def _lane_row(values, offset):
    row = jnp.zeros((1, 128), F32)
    return row.at[0, offset:offset + values.shape[0]].set(values.astype(F32))


def _rope_table(pos):
    half = MLA_ROPE // 2
    inv = jnp.exp(-math.log(ROPE_BASE) * jnp.arange(half, dtype=F32) / half)
    ang = pos.astype(F32)[:, None] * inv[None, :]
    cos, sin = jnp.cos(ang), jnp.sin(ang)
    return jnp.concatenate([cos, cos, sin, sin], axis=-1)


def kernel(x_prompt, x_sample, cache_mla_latent, cache_mla_rope, cache_fox_k, cache_fox_v, cache_fox_logf, state_gdn, state_gdn_conv, page_table, c_prompt, c_sample, w_ada, b_ada, g_norm1, g_norm2, w_in, gdn_conv_w, gdn_a_log, gdn_dt_bias, gdn_norm_g, mla_q_norm_g, mla_w_qb, mla_kv_norm_g, mla_w_kvb, fox_f_bias, w_out, ffn_w_gate, ffn_w_up, ffn_w_down, moe_router, moe_w_gate, moe_w_up, moe_w_down, g_final):
    n_layer = w_in.shape[0]
    bp, tp, d = x_prompt.shape
    bs, ts, _ = x_sample.shape
    n_pages, page = page_table.shape[1], cache_mla_latent.shape[2]
    past_len = n_pages * page
    assert tp % ROW_TILE == 0 and (bs * ts) % ROW_TILE == 0 and ROW_TILE % ts == 0
    assert tp >= GDN_CONV - 1 and ts >= GDN_CONV - 1 and n_pages % PAGES_PER_STEP == 0

    n_c = bp + bs
    c_all = jnp.concatenate([c_prompt, c_sample, jnp.zeros((-n_c % 8, d), F32)], axis=0)
    mod = _adaln(c_all, w_ada, b_ada)

    lp = jnp.swapaxes(cache_fox_logf, 2, 3)
    lp8 = jnp.concatenate([lp, lp], axis=2)
    pool = cache_fox_k.shape[1]
    fox_kt = jnp.transpose(cache_fox_k, (0, 1, 3, 4, 2)).reshape(n_layer, pool, FOX_H * FOX_HD, page)
    fox_vt = jnp.transpose(cache_fox_v, (0, 1, 3, 4, 2)).reshape(n_layer, pool, FOX_H * FOX_HD, page)
    mla_pe_t = jnp.swapaxes(cache_mla_rope, 2, 3)

    cs_p = _rope_table(jnp.arange(tp))
    cs_s = jnp.tile(_rope_table(past_len + jnp.arange(ts)), (bs, 1))
    head_mask = (jnp.arange(FOX_H)[:, None] == (jnp.arange(FOX_H * FOX_HD) // FOX_HD)[None, :])

    xp = x_prompt.reshape(bp * tp, d)
    xs = x_sample.reshape(bs * ts, d)
    outs = {k: [] for k in ("lat_p", "lat_s", "pe_p", "pe_s", "fk_p", "fk_s", "fv_p", "fv_s",
                            "lf_p", "lf_s", "st_p", "st_s", "cv_p", "cv_s")}

    for l in range(n_layer):
        w_in_l = _prep_w_in(w_in[l])
        w_qb_l = _prep_w_qb(mla_w_qb[l])
        w_kvb = mla_w_kvb[l].reshape(MLA_KVR, MLA_H, MLA_NOPE + MLA_V)
        w_kb_t = jnp.transpose(w_kvb[..., :MLA_NOPE], (1, 2, 0)).astype(BF16)
        w_vb_h = jnp.transpose(w_kvb[..., MLA_NOPE:], (1, 0, 2)).astype(BF16)
        w_vb_all = w_kvb[..., MLA_NOPE:].reshape(MLA_KVR, MLA_H * MLA_V).astype(BF16)
        w_out_l = w_out[l].astype(BF16)
        alog_row = _lane_row(gdn_a_log[l], SM_A)
        dtb_row = _lane_row(gdn_dt_bias[l], SM_A)
        fb_row = _lane_row(fox_f_bias[l], SM_F)
        gn = gdn_norm_g[l].reshape(1, GDN_DV)
        gq = mla_q_norm_g[l].reshape(1, MLA_QR)
        gkv = mla_kv_norm_g[l].reshape(1, MLA_KVR)
        g1 = g_norm1[l].reshape(1, d)
        g2 = g_norm2[l].reshape(1, d)
        cw = gdn_conv_w[l]
        i_ffn = l // 2
        last = l == n_layer - 1

        mods_p = [m.reshape(bp, 1, d) for m in jnp.split(mod[l, :bp], 6, axis=-1)]
        mods_s = [jnp.repeat(m, ts, axis=0) for m in jnp.split(mod[l, bp:bp + bs], 6, axis=-1)]

        def mixer_tail(x, mods, oa, ob, oc, seq_len):
            x = _out_proj(oa, ob, oc, w_out_l, x, mods[2], seq_len)
            gf = g_final.reshape(1, d) if last else None
            if l % 2 == 0:
                return _ffn(x, g2, mods[4], mods[3], mods[5], ffn_w_gate[i_ffn][None].astype(BF16),
                            ffn_w_up[i_ffn][None].astype(BF16), ffn_w_down[i_ffn][None].astype(BF16),
                            seq_len, g_final=gf)
            router = jnp.pad(moe_router[i_ffn], ((0, 0), (0, 128 - moe_router.shape[-1])))
            return _ffn(x, g2, mods[4], mods[3], mods[5], moe_w_gate[i_ffn].astype(BF16),
                        moe_w_up[i_ffn].astype(BF16), moe_w_down[i_ffn].astype(BF16),
                        seq_len, router=router, g_final=gf)

        y = _in_proj(xp, g1, mods_p[1], mods_p[0], w_in_l, tp)
        oa, st = _gdn_prompt(y, cw, alog_row, dtb_row, gn, bp, tp)
        qcat, kcat, lat, kpe = _mla_prep(y, cs_p, gq, gkv, w_qb_l, w_kb_t)
        ob = _flash(qcat, kcat, kcat, bp, tp, shared_kv=True, wp=w_vb_h, d_v=MLA_KVR, d_out=MLA_V)
        fq, fk, fv, fx = _fox_prep(y, fb_row, tp, True)
        g_cum = fx[:, 128 + SM_F:128 + SM_F + FOX_H]
        oc = _flash(fq, fk, fv, bp, tp, shared_kv=False, gq=g_cum.T[:, :, None], gk=g_cum.T[:, None, :],
                    d_v=FOX_HD, d_out=FOX_HD)
        y3 = y.reshape(bp, tp, N_Y)
        outs["lat_p"].append(lat.reshape(bp, tp, MLA_KVR))
        outs["pe_p"].append(kpe.reshape(bp, tp, MLA_ROPE))
        outs["fk_p"].append(y3[:, :, C_KC:C_KC + FOX_H * FOX_HD].reshape(bp, tp, FOX_H, FOX_HD))
        outs["fv_p"].append(y3[:, :, C_VC:C_VC + FOX_H * FOX_HD].reshape(bp, tp, FOX_H, FOX_HD))
        outs["lf_p"].append(fx[:, SM_F:SM_F + FOX_H].reshape(bp, tp, FOX_H))
        outs["st_p"].append(st)
        outs["cv_p"].append(y3[:, tp - (GDN_CONV - 1):, C_QKV:C_QKV + GDN_QKV])
        xp = mixer_tail(xp, mods_p, oa, ob, oc, tp)

        y = _in_proj(xs, g1, mods_s[1], mods_s[0], w_in_l, ts)
        y3 = y.reshape(bs, ts, N_Y)
        oa, st = _gdn_decode(y3, state_gdn_conv[l], state_gdn[l], cw, alog_row, dtb_row, gn)
        qcat, kcat, lat, kpe = _mla_prep(y, cs_s, gq, gkv, w_qb_l, w_kb_t)
        q_rows = jnp.transpose(qcat.reshape(MLA_H, bs, ts, 256), (1, 2, 0, 3)).reshape(bs, ts * MLA_H, 256)
        knew = jnp.pad(kcat.reshape(bs, ts, 256), ((0, 0), (0, 128 - ts), (0, 0)))
        ob = _mla_decode(page_table, q_rows, knew, w_vb_all, cache_mla_latent, mla_pe_t, l, ts)
        (fx,) = _fox_prep(y, fb_row, ts, False)
        g_rows = fx[:, 128 + SM_F:128 + SM_F + FOX_H].reshape(bs, ts * FOX_H, 1)
        qc3 = y3[:, :, C_QC:C_QC + FOX_H * FOX_HD]
        q_bd = jnp.where(head_mask[None, None], qc3[:, :, None, :], 0.0).reshape(bs, ts * FOX_H, FOX_H * FOX_HD)
        kc3 = y3[:, :, C_KC:C_KC + FOX_H * FOX_HD]
        vc3 = y3[:, :, C_VC:C_VC + FOX_H * FOX_HD]
        pad_new = ((0, 0), (0, 128 - ts), (0, 0))
        oc = _fox_decode(page_table, q_bd.astype(BF16), g_rows, jnp.pad(kc3, pad_new).astype(BF16),
                         jnp.pad(vc3, pad_new).astype(BF16), fox_kt, fox_vt, lp8, l, ts)
        outs["lat_s"].append(lat.reshape(bs, ts, MLA_KVR))
        outs["pe_s"].append(kpe.reshape(bs, ts, MLA_ROPE))
        outs["fk_s"].append(kc3.reshape(bs, ts, FOX_H, FOX_HD))
        outs["fv_s"].append(vc3.reshape(bs, ts, FOX_H, FOX_HD))
        outs["lf_s"].append(fx[:, SM_F:SM_F + FOX_H].reshape(bs, ts, FOX_H))
        outs["st_s"].append(st)
        outs["cv_s"].append(y3[:, ts - (GDN_CONV - 1):, C_QKV:C_QKV + GDN_QKV])
        xs = mixer_tail(xs, mods_s, oa.reshape(bs * ts, GDN_H * GDN_DV), ob.reshape(bs * ts, MLA_H * MLA_V),
                        oc.reshape(bs * ts, FOX_H * FOX_HD), ts)

    stack = lambda k: jnp.stack(outs[k], axis=0)
    return (xp.reshape(bp, tp, d), xs.reshape(bs, ts, d),
            stack("lat_p"), stack("lat_s"), stack("pe_p"), stack("pe_s"),
            stack("fk_p"), stack("fk_s"), stack("fv_p"), stack("fv_s"),
            stack("lf_p"), stack("lf_s"), stack("st_p"), stack("st_s"),
            stack("cv_p"), stack("cv_s"))
```

```python
import functools
import math

import jax
import jax.numpy as jnp
from jax import lax
from jax.experimental import pallas as pl
from jax.experimental.pallas import tpu as pltpu

F32 = jnp.float32
BF16 = jnp.bfloat16
EPS = 1e-6
NEG = -1e30

GDN_H, GDN_DK, GDN_DV, GDN_CONV = 4, 128, 128, 4
GDN_QKV = GDN_H * (2 * GDN_DK + GDN_DV)
MLA_H, MLA_QR, MLA_KVR, MLA_NOPE, MLA_ROPE, MLA_V = 4, 256, 128, 64, 32, 64
FOX_H, FOX_HD = 4, 64
ROPE_BASE = 10000.0
MLA_SCALE = (MLA_NOPE + MLA_ROPE) ** -0.5
FOX_SCALE = FOX_HD ** -0.5

C_QKV, C_Z, C_CQ, C_QC, C_KC, C_VC, C_CKV, C_SM, N_Y = 0, 1536, 2048, 2304, 2560, 2816, 3072, 3200, 3328
SM_KPE, SM_KROT, SM_B, SM_A, SM_F = 0, 32, 64, 68, 72

V7X_VMEM_LIMIT = 52 * 1024 * 1024
ROW_TILE = 512


def _cparams(*sem):
    return pltpu.CompilerParams(dimension_semantics=sem, vmem_limit_bytes=V7X_VMEM_LIMIT)


def _dot(a, b):
    return jnp.dot(a, b, preferred_element_type=F32)


def _dot_nt(a, b):
    return lax.dot_general(a, b, (((1,), (1,)), ((), ())), preferred_element_type=F32)


def _split2(a):
    hi = a.astype(BF16)
    return hi, (a - hi.astype(F32)).astype(BF16)


def _split3(a):
    hi = a.astype(BF16)
    r = a - hi.astype(F32)
    mid = r.astype(BF16)
    return hi, mid, (r - mid.astype(F32)).astype(BF16)


def _dot_hi(a, b):
    ah, al = _split2(a)
    bh, bl = _split2(b)
    return _dot(ah, bh) + _dot(ah, bl) + _dot(al, bh)


def _dot_ones_lhs(ones_bf16, x):
    h, m, l = _split3(x)
    return _dot(ones_bf16, h) + _dot(ones_bf16, m) + _dot(ones_bf16, l)


def _dot_ones_rhs(x, ones_bf16):
    h, m, l = _split3(x)
    return _dot(h, ones_bf16) + _dot(m, ones_bf16) + _dot(l, ones_bf16)


def _sigmoid(x):
    return 1.0 / (1.0 + jnp.exp(-x))


def _silu(x):
    return x * _sigmoid(x)


def _softplus(x):
    return jnp.maximum(x, 0.0) + jnp.log(1.0 + jnp.exp(-jnp.abs(x)))


def _log_sigmoid(x):
    return -_softplus(-x)


def _rms(x):
    return x * lax.rsqrt(jnp.mean(x * x, axis=-1, keepdims=True) + EPS)


def _iota2(shape, dim):
    return lax.broadcasted_iota(jnp.int32, shape, dim)


def _ada_kernel(c_ref, w_ref, b_ref, o_ref):
    a = _silu(c_ref[...]).astype(BF16)
    o_ref[...] = _dot(a, w_ref[...].astype(BF16)) + b_ref[...]


def _adaln(c_all, w_ada, b_ada):
    n_layer, d, n = w_ada.shape
    r = c_all.shape[0]
    tn = 1536
    return pl.pallas_call(
        _ada_kernel,
        grid=(n_layer, n // tn),
        in_specs=[pl.BlockSpec((r, d), lambda l, j: (0, 0)),
                  pl.BlockSpec((None, d, tn), lambda l, j: (l, 0, j)),
                  pl.BlockSpec((None, 1, tn), lambda l, j: (l, 0, j))],
        out_specs=pl.BlockSpec((None, r, tn), lambda l, j: (l, 0, j)),
        out_shape=jax.ShapeDtypeStruct((n_layer, r, n), F32),
        compiler_params=_cparams("arbitrary", "arbitrary"),
        name="adaln",
    )(c_all, w_ada, b_ada.reshape(n_layer, 1, n))


def _mod_spec(tm, d, seq_len):
    if seq_len >= tm:
        return pl.BlockSpec((None, 1, d), lambda i, *_: ((i * tm) // seq_len, 0, 0))
    return pl.BlockSpec((tm, d), lambda i, *_: (i, 0))


def _in_proj_kernel(x_ref, g_ref, sc_ref, sh_ref, w_ref, o_ref):
    h = _rms(x_ref[...]) * g_ref[...]
    h = h * (1.0 + sc_ref[...]) + sh_ref[...]
    o_ref[...] = _dot(h.astype(BF16), w_ref[...])


def _in_proj(x, g, sc, sh, w, seq_len):
    r, d = x.shape
    n = w.shape[1]
    tm = 256
    ms = _mod_spec(tm, d, seq_len)
    return pl.pallas_call(
        _in_proj_kernel,
        grid=(r // tm,),
        in_specs=[pl.BlockSpec((tm, d), lambda i: (i, 0)),
                  pl.BlockSpec((1, d), lambda i: (0, 0)), ms, ms,
                  pl.BlockSpec((d, n), lambda i: (0, 0))],
        out_specs=pl.BlockSpec((tm, n), lambda i: (i, 0)),
        out_shape=jax.ShapeDtypeStruct((r, n), F32),
        compiler_params=_cparams("parallel"),
        name="in_proj",
    )(x, g, sc, sh, w)


def _gdn_prompt_kernel(qkv_ref, z_ref, sm_ref, cw_ref, alog_ref, dtb_ref, gn_ref, o_ref, so_ref,
                       ext_ref, s_ref, *, chunk, n_chunk, n_seq):
    c = pl.program_id(1)

    @pl.when(c == 0)
    def _():
        ext_ref[:, 0:8, :] = jnp.zeros((n_seq, 8, GDN_QKV), F32)
        s_ref[...] = jnp.zeros_like(s_ref)

    for i in range(n_seq):
        _gdn_chunk(qkv_ref.at[i], z_ref.at[i], sm_ref.at[i], cw_ref, alog_ref, dtb_ref, gn_ref,
                   o_ref.at[i], ext_ref.at[i], s_ref.at[i], chunk)

    @pl.when(c == n_chunk - 1)
    def _():
        so_ref[...] = s_ref[...]


def _gdn_chunk(qkv_ref, z_ref, sm_ref, cw_ref, alog_ref, dtb_ref, gn_ref, o_ref, ext_ref, s_ref, chunk):
    n_h, dk = GDN_H, GDN_DK
    ext_ref[8:8 + chunk, :] = qkv_ref[...]
    cw = cw_ref[...]
    conv = cw[0:1, :] * ext_ref[5:5 + chunk, :]
    for j in range(1, GDN_CONV):
        conv = conv + cw[j:j + 1, :] * ext_ref[5 + j:5 + j + chunk, :]
    ext_ref[0:8, :] = ext_ref[chunk:chunk + 8, :]
    act = _silu(conv)

    sm = sm_ref[...]
    g_log = -jnp.exp(alog_ref[...]) * _softplus(sm + dtb_ref[...])
    beta_all = _sigmoid(sm)
    row = _iota2((chunk, chunk), 0)
    col = _iota2((chunk, chunk), 1)
    incl = row >= col
    strict = row > col
    tri = jnp.where(incl, 1.0, 0.0).astype(BF16)
    eye = jnp.where(row == col, 1.0, 0.0)
    gc_all = _dot_ones_lhs(tri, g_log)
    gc_t = gc_all.T

    for h in range(n_h):
        q = act[:, h * dk:(h + 1) * dk]
        k = act[:, (n_h + h) * dk:(n_h + h + 1) * dk]
        v = act[:, 2 * n_h * dk + h * GDN_DV:2 * n_h * dk + (h + 1) * GDN_DV]
        q = q * lax.rsqrt(jnp.sum(q * q, axis=-1, keepdims=True) + EPS) * (dk ** -0.5)
        k = k * lax.rsqrt(jnp.sum(k * k, axis=-1, keepdims=True) + EPS)
        beta = beta_all[:, SM_B + h:SM_B + h + 1]
        gcol = gc_all[:, SM_A + h:SM_A + h + 1]
        grow = gc_t[SM_A + h:SM_A + h + 1, :]
        decay = jnp.where(incl, jnp.exp(jnp.where(incl, gcol - grow, 0.0)), 0.0)
        kb = k * beta
        vb = v * beta
        kbf = k.astype(BF16)
        x = -jnp.where(strict, _dot_nt(kb.astype(BF16), kbf) * decay, 0.0)
        minv = eye + x
        p = x
        for _ in range(int(math.log2(chunk)) - 1):
            p = _dot_hi(p, p)
            minv = minv + _dot_hi(minv, p)
        u = _dot_hi(minv, vb)
        w = _dot_hi(minv, kb * jnp.exp(gcol))
        aqk = _dot_nt(q.astype(BF16), kbf) * decay
        s_h = s_ref[h]
        s_bf = s_h.astype(BF16)
        v_new = u - _dot(w.astype(BF16), s_bf)
        o = _dot((q * jnp.exp(gcol)).astype(BF16), s_bf) + _dot(aqk.astype(BF16), v_new.astype(BF16))
        g_last = gcol[chunk - 1:chunk, :]
        k_dec_t = (k * jnp.exp(g_last - gcol)).T
        s_ref[h] = s_h * jnp.exp(g_last) + _dot(k_dec_t.astype(BF16), v_new.astype(BF16))
        o = _rms(o) * gn_ref[...] * _silu(z_ref[:, h * GDN_DV:(h + 1) * GDN_DV])
        o_ref[:, h * GDN_DV:(h + 1) * GDN_DV] = o.astype(o_ref.dtype)


def _gdn_prompt(y, cw, alog_row, dtb_row, gn, batch, seq_len):
    chunk = 128
    n_chunk = seq_len // chunk
    n_seq = 2 if batch % 2 == 0 else 1
    y3 = y.reshape(batch, seq_len, y.shape[1])
    kern = functools.partial(_gdn_prompt_kernel, chunk=chunk, n_chunk=n_chunk, n_seq=n_seq)
    o, st = pl.pallas_call(
        kern,
        grid=(batch // n_seq, n_chunk),
        in_specs=[pl.BlockSpec((n_seq, chunk, GDN_QKV), lambda b, c: (b, c, C_QKV // GDN_QKV)),
                  pl.BlockSpec((n_seq, chunk, 512), lambda b, c: (b, c, C_Z // 512)),
                  pl.BlockSpec((n_seq, chunk, 128), lambda b, c: (b, c, C_SM // 128)),
                  pl.BlockSpec((GDN_CONV, GDN_QKV), lambda b, c: (0, 0)),
                  pl.BlockSpec((1, 128), lambda b, c: (0, 0)),
                  pl.BlockSpec((1, 128), lambda b, c: (0, 0)),
                  pl.BlockSpec((1, GDN_DV), lambda b, c: (0, 0))],
        out_specs=[pl.BlockSpec((n_seq, chunk, GDN_H * GDN_DV), lambda b, c: (b, c, 0)),
                   pl.BlockSpec((n_seq, GDN_H, GDN_DK, GDN_DV), lambda b, c: (b, 0, 0, 0))],
        out_shape=[jax.ShapeDtypeStruct((batch, seq_len, GDN_H * GDN_DV), BF16),
                   jax.ShapeDtypeStruct((batch, GDN_H, GDN_DK, GDN_DV), F32)],
        scratch_shapes=[pltpu.VMEM((n_seq, chunk + 8, GDN_QKV), F32),
                        pltpu.VMEM((n_seq, GDN_H, GDN_DK, GDN_DV), F32)],
        compiler_params=_cparams("arbitrary", "arbitrary"),
        name="gdn_prompt",
    )(y3, y3, y3, cw, alog_row, dtb_row, gn)
    return o.reshape(batch * seq_len, GDN_H * GDN_DV), st


def _gdn_decode_kernel(qkv_ref, z_ref, sm_ref, buf_ref, s0_ref, cw_ref, alog_ref, dtb_ref, gn_ref,
                       o_ref, so_ref, ext_ref, qk_ref, *, n_tok):
    n_h, dk = GDN_H, GDN_DK
    ext_ref[0:GDN_CONV - 1, :] = buf_ref[...]
    ext_ref[GDN_CONV - 1:GDN_CONV - 1 + n_tok, :] = qkv_ref[...]
    cw = cw_ref[...]
    conv = cw[0:1, :] * ext_ref[0:n_tok, :]
    for j in range(1, GDN_CONV):
        conv = conv + cw[j:j + 1, :] * ext_ref[j:j + n_tok, :]
    act = _silu(conv)
    sm = sm_ref[...]
    decay_all = jnp.exp(-jnp.exp(alog_ref[...]) * _softplus(sm + dtb_ref[...]))
    beta_all = _sigmoid(sm)

    qk_ref[...] = jnp.zeros_like(qk_ref)
    for h in range(n_h):
        q = act[:, h * dk:(h + 1) * dk]
        k = act[:, (n_h + h) * dk:(n_h + h + 1) * dk]
        q = q * lax.rsqrt(jnp.sum(q * q, axis=-1, keepdims=True) + EPS) * (dk ** -0.5)
        k = k * lax.rsqrt(jnp.sum(k * k, axis=-1, keepdims=True) + EPS)
        qk_ref[h * n_tok:(h + 1) * n_tok, :] = q
        qk_ref[(n_h + h) * n_tok:(n_h + h + 1) * n_tok, :] = k
    qk_t = qk_ref[...].T

    for h in range(n_h):
        s = s0_ref[h]
        for t in range(n_tok):
            q_col = qk_t[:, h * n_tok + t:h * n_tok + t + 1]
            k_col = qk_t[:, (n_h + h) * n_tok + t:(n_h + h) * n_tok + t + 1]
            v_row = act[t:t + 1, 2 * n_h * dk + h * GDN_DV:2 * n_h * dk + (h + 1) * GDN_DV]
            s = s * decay_all[t:t + 1, SM_A + h:SM_A + h + 1]
            pred = jnp.sum(k_col * s, axis=0, keepdims=True)
            v_new = beta_all[t:t + 1, SM_B + h:SM_B + h + 1] * (v_row - pred)
            s = s + k_col * v_new
            o = jnp.sum(q_col * s, axis=0, keepdims=True)
            o = _rms(o) * gn_ref[...] * _silu(z_ref[t:t + 1, h * GDN_DV:(h + 1) * GDN_DV])
            o_ref[t:t + 1, h * GDN_DV:(h + 1) * GDN_DV] = o.astype(o_ref.dtype)
        so_ref[h] = s


def _gdn_decode(y3, buf, s0, cw, alog_row, dtb_row, gn):
    batch, n_tok, _ = y3.shape
    kern = functools.partial(_gdn_decode_kernel, n_tok=n_tok)
    return pl.pallas_call(
        kern,
        grid=(batch,),
        in_specs=[pl.BlockSpec((None, n_tok, GDN_QKV), lambda b: (b, 0, C_QKV // GDN_QKV)),
                  pl.BlockSpec((None, n_tok, 512), lambda b: (b, 0, C_Z // 512)),
                  pl.BlockSpec((None, n_tok, 128), lambda b: (b, 0, C_SM // 128)),
                  pl.BlockSpec((None, GDN_CONV - 1, GDN_QKV), lambda b: (b, 0, 0)),
                  pl.BlockSpec((None, GDN_H, GDN_DK, GDN_DV), lambda b: (b, 0, 0, 0)),
                  pl.BlockSpec((GDN_CONV, GDN_QKV), lambda b: (0, 0)),
                  pl.BlockSpec((1, 128), lambda b: (0, 0)),
                  pl.BlockSpec((1, 128), lambda b: (0, 0)),
                  pl.BlockSpec((1, GDN_DV), lambda b: (0, 0))],
        out_specs=[pl.BlockSpec((None, n_tok, GDN_H * GDN_DV), lambda b: (b, 0, 0)),
                   pl.BlockSpec((None, GDN_H, GDN_DK, GDN_DV), lambda b: (b, 0, 0, 0))],
        out_shape=[jax.ShapeDtypeStruct((batch, n_tok, GDN_H * GDN_DV), F32),
                   jax.ShapeDtypeStruct((batch, GDN_H, GDN_DK, GDN_DV), F32)],
        scratch_shapes=[pltpu.VMEM((16, GDN_QKV), F32), pltpu.VMEM((128, GDN_DK), F32)],
        compiler_params=_cparams("arbitrary"),
        name="gdn_decode",
    )(y3, y3, y3, buf, s0, cw, alog_row, dtb_row, gn)


def _mla_prep_kernel(cq_ref, ckv_ref, sm_ref, cs_ref, gq_ref, gkv_ref, wq_ref, wkb_ref,
                     qcat_ref, kcat_ref, lat_ref, kpe_ref):
    tm = cq_ref.shape[0]
    cs = cs_ref[...]
    cos = cs[:, 0:MLA_ROPE]
    sin = cs[:, MLA_ROPE:2 * MLA_ROPE]
    lat = _rms(ckv_ref[...]) * gkv_ref[...]
    sm = sm_ref[...]
    kpe = sm[:, SM_KPE:SM_KPE + MLA_ROPE] * cos + sm[:, SM_KROT:SM_KROT + MLA_ROPE] * sin
    lat_ref[...] = lat
    kpe_ref[...] = kpe
    pad = jnp.zeros((tm, 256 - MLA_KVR - MLA_ROPE), F32)
    kcat_ref[:, 0:MLA_KVR] = lat.astype(kcat_ref.dtype)
    kcat_ref[:, MLA_KVR:MLA_KVR + MLA_ROPE] = kpe.astype(kcat_ref.dtype)
    kcat_ref[:, MLA_KVR + MLA_ROPE:256] = pad.astype(kcat_ref.dtype)

    qn = _rms(cq_ref[...]) * gq_ref[...]
    qb = _dot(qn.astype(BF16), wq_ref[...])
    for h in range(MLA_H):
        base = h * 128
        nope = qb[:, base:base + MLA_NOPE]
        pe = (qb[:, base + MLA_NOPE:base + MLA_NOPE + MLA_ROPE] * cos
              + qb[:, base + MLA_NOPE + MLA_ROPE:base + 128] * sin)
        q_abs = _dot(nope.astype(BF16), wkb_ref[h])
        qcat_ref[h, :, 0:MLA_KVR] = (q_abs * MLA_SCALE).astype(qcat_ref.dtype)
        qcat_ref[h, :, MLA_KVR:MLA_KVR + MLA_ROPE] = (pe * MLA_SCALE).astype(qcat_ref.dtype)
        qcat_ref[h, :, MLA_KVR + MLA_ROPE:256] = pad.astype(qcat_ref.dtype)


def _mla_prep(y, cs, gq, gkv, wq, wkb):
    r = y.shape[0]
    tm = ROW_TILE
    n_tab = cs.shape[0] // tm
    return pl.pallas_call(
        _mla_prep_kernel,
        grid=(r // tm,),
        in_specs=[pl.BlockSpec((tm, MLA_QR), lambda i: (i, C_CQ // MLA_QR)),
                  pl.BlockSpec((tm, MLA_KVR), lambda i: (i, C_CKV // MLA_KVR)),
                  pl.BlockSpec((tm, 128), lambda i: (i, C_SM // 128)),
                  pl.BlockSpec((tm, 2 * MLA_ROPE), lambda i: (i % n_tab, 0)),
                  pl.BlockSpec((1, MLA_QR), lambda i: (0, 0)),
                  pl.BlockSpec((1, MLA_KVR), lambda i: (0, 0)),
                  pl.BlockSpec((MLA_QR, MLA_H * 128), lambda i: (0, 0)),
                  pl.BlockSpec((MLA_H, MLA_NOPE, MLA_KVR), lambda i: (0, 0, 0))],
        out_specs=[pl.BlockSpec((MLA_H, tm, 256), lambda i: (0, i, 0)),
                   pl.BlockSpec((tm, 256), lambda i: (i, 0)),
                   pl.BlockSpec((tm, MLA_KVR), lambda i: (i, 0)),
                   pl.BlockSpec((tm, MLA_ROPE), lambda i: (i, 0))],
        out_shape=[jax.ShapeDtypeStruct((MLA_H, r, 256), BF16),
                   jax.ShapeDtypeStruct((r, 256), BF16),
                   jax.ShapeDtypeStruct((r, MLA_KVR), F32),
                   jax.ShapeDtypeStruct((r, MLA_ROPE), F32)],
        compiler_params=_cparams("parallel"),
        name="mla_prep",
    )(y, y, y, cs, gq, gkv, wq, wkb)


def _fox_prep_kernel(*refs, seq_len, tm, emit_heads):
    if emit_heads:
        qc_ref, kc_ref, vc_ref, sm_ref, fb_ref, q_ref, k_ref, v_ref, fx_ref, carry_ref = refs
    else:
        sm_ref, fb_ref, fx_ref, carry_ref = refs
    i = pl.program_id(0)
    logf = _log_sigmoid(sm_ref[...] + fb_ref[...])
    row = _iota2((tm, tm), 0)
    col = _iota2((tm, tm), 1)
    if seq_len >= tm:
        @pl.when((i * tm) % seq_len == 0)
        def _():
            carry_ref[...] = jnp.zeros_like(carry_ref)
        tri = jnp.where(row >= col, 1.0, 0.0).astype(BF16)
        g = _dot_ones_lhs(tri, logf) + carry_ref[...]
        carry_ref[...] = g[tm - 1:tm, :]
    else:
        same = (row // seq_len) == (col // seq_len)
        tri = jnp.where((row >= col) & same, 1.0, 0.0).astype(BF16)
        g = _dot_ones_lhs(tri, logf)
    fx_ref[:, 0:128] = logf
    fx_ref[:, 128:256] = g
    if emit_heads:
        for h in range(FOX_H):
            sl = slice(h * FOX_HD, (h + 1) * FOX_HD)
            q_ref[h] = (qc_ref[:, sl] * FOX_SCALE).astype(BF16)
            k_ref[h] = kc_ref[:, sl].astype(BF16)
            v_ref[h] = vc_ref[:, sl].astype(BF16)


def _fox_prep(y, fb_row, seq_len, emit_heads):
    r = y.shape[0]
    tm = ROW_TILE
    kern = functools.partial(_fox_prep_kernel, seq_len=seq_len, tm=tm, emit_heads=emit_heads)
    sm_spec = pl.BlockSpec((tm, 128), lambda i: (i, C_SM // 128))
    fb_spec = pl.BlockSpec((1, 128), lambda i: (0, 0))
    fx_spec = pl.BlockSpec((tm, 256), lambda i: (i, 0))
    fx_shape = jax.ShapeDtypeStruct((r, 256), F32)
    w = FOX_H * FOX_HD
    if emit_heads:
        head_spec = pl.BlockSpec((FOX_H, tm, FOX_HD), lambda i: (0, i, 0))
        head_shape = jax.ShapeDtypeStruct((FOX_H, r, FOX_HD), BF16)
        in_specs = [pl.BlockSpec((tm, w), lambda i: (i, C_QC // w)),
                    pl.BlockSpec((tm, w), lambda i: (i, C_KC // w)),
                    pl.BlockSpec((tm, w), lambda i: (i, C_VC // w)), sm_spec, fb_spec]
        args = (y, y, y, y, fb_row)
        out_specs = [head_spec, head_spec, head_spec, fx_spec]
        out_shape = [head_shape, head_shape, head_shape, fx_shape]
    else:
        in_specs = [sm_spec, fb_spec]
        args = (y, fb_row)
        out_specs = [fx_spec]
        out_shape = [fx_shape]
    return pl.pallas_call(
        kern, grid=(r // tm,), in_specs=in_specs, out_specs=out_specs, out_shape=out_shape,
        scratch_shapes=[pltpu.VMEM((1, 128), F32)],
        compiler_params=_cparams("arbitrary"),
        name="fox_prep",
    )(*args)


def _flash_kernel(*refs, n_head, shared_kv, has_bias, has_proj, d_v, d_out):
    refs = list(refs)
    q_ref, k_ref, v_ref = refs[:3]
    pos = 3
    if has_bias:
        gq_ref, gk_ref = refs[pos:pos + 2]
        pos += 2
    if has_proj:
        wp_ref = refs[pos]
        pos += 1
    o_ref, m_ref, l_ref, acc_ref = refs[pos:pos + 4]
    qi = pl.program_id(1)
    ki = pl.program_id(2)
    tq = q_ref.shape[1]
    tk = k_ref.shape[-2]

    @pl.when(ki == 0)
    def _():
        m_ref[...] = jnp.full(m_ref.shape, NEG, F32)
        l_ref[...] = jnp.zeros_like(l_ref)
        acc_ref[...] = jnp.zeros_like(acc_ref)

    def update(masked):
        for h in range(n_head):
            k = k_ref[...] if shared_kv else k_ref[h]
            v = v_ref[...] if shared_kv else v_ref[h]
            s = _dot_nt(q_ref[h], k)
            if has_bias:
                s = s + gq_ref[h] - gk_ref[h]
            if masked:
                s = jnp.where(_iota2((tq, tk), 1) <= _iota2((tq, tk), 0), s, NEG)
            m_prev = m_ref[h]
            m_new = jnp.maximum(m_prev, jnp.max(s, axis=-1, keepdims=True))
            alpha = jnp.exp(m_prev - m_new)
            p = jnp.exp(s - m_new)
            l_ref[h] = alpha * l_ref[h] + jnp.sum(p, axis=-1, keepdims=True)
            acc_ref[h] = alpha * acc_ref[h] + _dot(p.astype(BF16), v)
            m_ref[h] = m_new

    @pl.when(ki < qi)
    def _():
        update(False)

    @pl.when(ki == qi)
    def _():
        update(True)
        for h in range(n_head):
            o = acc_ref[h] / l_ref[h]
            if has_proj:
                o = _dot(o.astype(BF16), wp_ref[h])
            o_ref[:, h * d_out:(h + 1) * d_out] = o.astype(o_ref.dtype)


def _flash(q, k, v, batch, seq_len, *, shared_kv, gq=None, gk=None, wp=None, d_v, d_out):
    n_head, r, d_k = q.shape
    t = ROW_TILE
    nb = seq_len // t
    has_bias = gq is not None
    has_proj = wp is not None
    kern = functools.partial(_flash_kernel, n_head=n_head, shared_kv=shared_kv, has_bias=has_bias,
                             has_proj=has_proj, d_v=d_v, d_out=d_out)
    q_spec = pl.BlockSpec((n_head, t, d_k), lambda b, i, j: (0, b * nb + i, 0))
    if shared_kv:
        k_spec = pl.BlockSpec((t, d_k), lambda b, i, j: (b * nb + jnp.minimum(i, j), 0))
        v_spec = pl.BlockSpec((t, d_v), lambda b, i, j: (b * nb + jnp.minimum(i, j), 0))
    else:
        k_spec = pl.BlockSpec((n_head, t, d_k), lambda b, i, j: (0, b * nb + jnp.minimum(i, j), 0))
        v_spec = pl.BlockSpec((n_head, t, d_v), lambda b, i, j: (0, b * nb + jnp.minimum(i, j), 0))
    in_specs = [q_spec, k_spec, v_spec]
    args = [q, k, v]
    if has_bias:
        in_specs += [pl.BlockSpec((n_head, t, 1), lambda b, i, j: (0, b * nb + i, 0)),
                     pl.BlockSpec((n_head, 1, t), lambda b, i, j: (0, 0, b * nb + jnp.minimum(i, j)))]
        args += [gq, gk]
    if has_proj:
        in_specs.append(pl.BlockSpec(wp.shape, lambda b, i, j: (0, 0, 0)))
        args.append(wp)
    return pl.pallas_call(
        kern,
        grid=(batch, nb, nb),
        in_specs=in_specs,
        out_specs=pl.BlockSpec((t, n_head * d_out), lambda b, i, j: (b * nb + i, 0)),
        out_shape=jax.ShapeDtypeStruct((r, n_head * d_out), BF16),
        scratch_shapes=[pltpu.VMEM((n_head, t, 1), F32), pltpu.VMEM((n_head, t, 1), F32),
                        pltpu.VMEM((n_head, t, d_v), F32)],
        compiler_params=_cparams("parallel", "arbitrary", "arbitrary"),
        name="flash_shared" if shared_kv else "flash_heads",
    )(*args)


PAGES_PER_STEP = 16


def _softmax_step(s, m_ref, l_ref):
    m_prev = m_ref[...]
    m_new = jnp.maximum(m_prev, jnp.max(s, axis=-1, keepdims=True))
    alpha = jnp.exp(m_prev - m_new)
    p = jnp.exp(s - m_new)
    l_ref[...] = alpha * l_ref[...] + jnp.sum(p, axis=-1, keepdims=True)
    m_ref[...] = m_new
    return alpha, p


def _new_token_mask(n_rows, n_keys, n_tok, n_head):
    t = _iota2((n_rows, n_keys), 0) // n_head
    j = _iota2((n_rows, n_keys), 1)
    return (j <= t) & (j < n_tok)


def _page_copies(pt_ref, b, c, slot, srcs, bufs, sem, *, layer, pps, n_pages, reverse, start):
    for j in range(pps):
        logical = c * pps + j
        pid = pt_ref[b, (n_pages - 1 - logical) if reverse else logical]
        for src, buf in zip(srcs, bufs):
            cp = pltpu.make_async_copy(src.at[layer, pid], buf.at[slot, j], sem.at[slot])
            if start:
                cp.start()
            else:
                cp.wait()


def _paged_fetch(fetch, n_step):
    b = pl.program_id(0)
    c = pl.program_id(1)
    g = b * n_step + c
    slot = g % 2

    @pl.when(g == 0)
    def _():
        fetch(b, c, slot, start=True)

    @pl.when(g + 1 < pl.num_programs(0) * n_step)
    def _():
        wrap = c + 1 == n_step
        fetch(jnp.where(wrap, b + 1, b), jnp.where(wrap, 0, c + 1), 1 - slot, start=True)

    fetch(b, c, slot, start=False)
    return slot


def _mla_decode_kernel(pt_ref, q_ref, knew_ref, wvb_ref, lat_hbm, pe_hbm, o_ref, m_ref, l_ref, acc_ref,
                       lat_buf, pe_buf, sem, *, layer, pps, n_step, n_tok):
    c = pl.program_id(1)
    n_rows = q_ref.shape[0]
    fetch = functools.partial(_page_copies, pt_ref, srcs=(lat_hbm, pe_hbm), bufs=(lat_buf, pe_buf), sem=sem,
                              layer=layer, pps=pps, n_pages=n_step * pps, reverse=False)
    slot = _paged_fetch(fetch, n_step)

    @pl.when(c == 0)
    def _():
        m_ref[...] = jnp.full(m_ref.shape, NEG, F32)
        l_ref[...] = jnp.zeros_like(l_ref)
        acc_ref[...] = jnp.zeros_like(acc_ref)

    q = q_ref[...]
    q_abs = q[:, 0:MLA_KVR]
    q_pe = q[:, MLA_KVR:MLA_KVR + MLA_ROPE]
    page = lat_buf.shape[2]
    lat = lat_buf[slot].reshape(pps * page, MLA_KVR).astype(BF16)
    pe_t = jnp.concatenate([pe_buf[slot, j].astype(BF16) for j in range(pps)], axis=1)
    s = _dot_nt(q_abs, lat) + _dot(q_pe, pe_t)
    alpha, p = _softmax_step(s, m_ref, l_ref)
    acc_ref[...] = alpha * acc_ref[...] + _dot(p.astype(BF16), lat)

    @pl.when(c == n_step - 1)
    def _():
        kn = knew_ref[...]
        s_new = _dot_nt(q, kn)
        s_new = jnp.where(_new_token_mask(n_rows, kn.shape[0], n_tok, MLA_H), s_new, NEG)
        alpha2, p2 = _softmax_step(s_new, m_ref, l_ref)
        o = (alpha2 * acc_ref[...] + _dot(p2.astype(BF16), kn[:, 0:MLA_KVR])) / l_ref[...]
        full = _dot(o.astype(BF16), wvb_ref[...])
        head = _iota2((n_rows, MLA_V), 0) % MLA_H
        out = jnp.zeros((n_rows, MLA_V), F32)
        for h in range(MLA_H):
            out = out + jnp.where(head == h, full[:, h * MLA_V:(h + 1) * MLA_V], 0.0)
        o_ref[...] = out.astype(o_ref.dtype)


def _mla_decode(page_table, q, knew, wvb, cache_lat, cache_pe_t, layer, n_tok):
    batch, n_rows, _ = q.shape
    n_pages = page_table.shape[1]
    pps = PAGES_PER_STEP
    n_step = n_pages // pps
    kern = functools.partial(_mla_decode_kernel, layer=layer, pps=pps, n_step=n_step, n_tok=n_tok)
    in_specs = [pl.BlockSpec((None, n_rows, 256), lambda b, c, pt: (b, 0, 0)),
                pl.BlockSpec((None, knew.shape[1], 256), lambda b, c, pt: (b, 0, 0)),
                pl.BlockSpec(wvb.shape, lambda b, c, pt: (0, 0)),
                pl.BlockSpec(memory_space=pl.ANY), pl.BlockSpec(memory_space=pl.ANY)]
    return pl.pallas_call(
        kern,
        grid_spec=pltpu.PrefetchScalarGridSpec(
            num_scalar_prefetch=1, grid=(batch, n_step), in_specs=in_specs,
            out_specs=pl.BlockSpec((None, n_rows, MLA_V), lambda b, c, pt: (b, 0, 0)),
            scratch_shapes=[pltpu.VMEM((n_rows, 1), F32), pltpu.VMEM((n_rows, 1), F32),
                            pltpu.VMEM((n_rows, MLA_KVR), F32),
                            pltpu.VMEM((2, pps) + cache_lat.shape[2:], F32),
                            pltpu.VMEM((2, pps) + cache_pe_t.shape[2:], F32),
                            pltpu.SemaphoreType.DMA((2,))]),
        out_shape=jax.ShapeDtypeStruct((batch, n_rows, MLA_V), BF16),
        compiler_params=_cparams("arbitrary", "arbitrary"),
        name="mla_decode",
    )(page_table, q, knew, wvb, cache_lat, cache_pe_t)


def _fox_decode_kernel(pt_ref, q_ref, g_ref, knew_ref, vnew_ref, k_hbm, v_hbm, lp_hbm, o_ref, m_ref, l_ref,
                       acc_ref, tot_ref, k_buf, v_buf, lp_buf, sem, *, layer, pps, n_step, n_tok):
    c = pl.program_id(1)
    n_rows = q_ref.shape[0]
    n_h, hd = FOX_H, FOX_HD
    page = lp_buf.shape[-1]
    fetch = functools.partial(_page_copies, pt_ref, srcs=(k_hbm, v_hbm, lp_hbm), bufs=(k_buf, v_buf, lp_buf),
                              sem=sem, layer=layer, pps=pps, n_pages=n_step * pps, reverse=True)
    slot = _paged_fetch(fetch, n_step)

    @pl.when(c == 0)
    def _():
        m_ref[...] = jnp.full(m_ref.shape, NEG, F32)
        l_ref[...] = jnp.zeros_like(l_ref)
        acc_ref[...] = jnp.zeros_like(acc_ref)
        tot_ref[...] = jnp.zeros_like(tot_ref)

    q = q_ref[...]
    later = jnp.where(_iota2((page, page), 0) >= _iota2((page, page), 1), 1.0, 0.0).astype(BF16)
    k_t = jnp.concatenate([k_buf[slot, j].astype(BF16) for j in range(pps)], axis=1)
    v_t = jnp.concatenate([v_buf[slot, j].astype(BF16) for j in range(pps)], axis=1)
    lp_all = lp_buf[slot].reshape(pps * 2 * n_h, page)
    rev_all = _dot_ones_rhs(lp_all, later)
    tot = tot_ref[...]
    parts = []
    for j in range(pps):
        rev = rev_all[j * 2 * n_h:(j + 1) * 2 * n_h, :]
        parts.append(tot + rev - lp_all[j * 2 * n_h:(j + 1) * 2 * n_h, :])
        tot = tot + rev[:, 0:1]
    tot_ref[...] = tot
    suffix = jnp.concatenate(parts, axis=1)
    s = (_dot(q, k_t) * FOX_SCALE + jnp.concatenate([suffix] * (n_rows // (2 * n_h)), axis=0) + g_ref[...])
    alpha, p = _softmax_step(s, m_ref, l_ref)
    acc_ref[...] = alpha * acc_ref[...] + _dot_nt(p.astype(BF16), v_t)

    @pl.when(c == n_step - 1)
    def _():
        kn = knew_ref[...]
        gcol = g_ref[...]
        s_new = _dot_nt(q, kn) * FOX_SCALE
        rows_i = _iota2((n_rows, kn.shape[0]), 0)
        cols_j = _iota2((n_rows, kn.shape[0]), 1)
        gk = jnp.zeros((n_rows, kn.shape[0]), F32)
        for j in range(n_tok):
            for h in range(n_h):
                gj = gcol[j * n_h + h:j * n_h + h + 1, :]
                gk = gk + jnp.where((cols_j == j) & (rows_i % n_h == h), gj, 0.0)
        s_new = s_new + gcol - gk
        s_new = jnp.where(_new_token_mask(n_rows, kn.shape[0], n_tok, n_h), s_new, NEG)
        alpha2, p2 = _softmax_step(s_new, m_ref, l_ref)
        full = alpha2 * acc_ref[...] + _dot(p2.astype(BF16), vnew_ref[...])
        head = _iota2((n_rows, hd), 0) % n_h
        out = jnp.zeros((n_rows, hd), F32)
        for h in range(n_h):
            out = out + jnp.where(head == h, full[:, h * hd:(h + 1) * hd], 0.0)
        o_ref[...] = (out / l_ref[...]).astype(o_ref.dtype)


def _fox_decode(page_table, q, gcol, knew, vnew, cache_kt, cache_vt, lp8, layer, n_tok):
    batch, n_rows, _ = q.shape
    n_pages = page_table.shape[1]
    pps = PAGES_PER_STEP
    n_step = n_pages // pps
    page = cache_kt.shape[3]
    kern = functools.partial(_fox_decode_kernel, layer=layer, pps=pps, n_step=n_step, n_tok=n_tok)
    w = FOX_H * FOX_HD
    hbm = pl.BlockSpec(memory_space=pl.ANY)
    in_specs = [pl.BlockSpec((None, n_rows, w), lambda b, c, pt: (b, 0, 0)),
                pl.BlockSpec((None, n_rows, 1), lambda b, c, pt: (b, 0, 0)),
                pl.BlockSpec((None, knew.shape[1], w), lambda b, c, pt: (b, 0, 0)),
                pl.BlockSpec((None, vnew.shape[1], w), lambda b, c, pt: (b, 0, 0)),
                hbm, hbm, hbm]
    return pl.pallas_call(
        kern,
        grid_spec=pltpu.PrefetchScalarGridSpec(
            num_scalar_prefetch=1, grid=(batch, n_step), in_specs=in_specs,
            out_specs=pl.BlockSpec((None, n_rows, FOX_HD), lambda b, c, pt: (b, 0, 0)),
            scratch_shapes=[pltpu.VMEM((n_rows, 1), F32), pltpu.VMEM((n_rows, 1), F32),
                            pltpu.VMEM((n_rows, FOX_H * FOX_HD), F32), pltpu.VMEM((2 * FOX_H, page), F32),
                            pltpu.VMEM((2, pps, w, page), F32), pltpu.VMEM((2, pps, w, page), F32),
                            pltpu.VMEM((2, pps, 2 * FOX_H, page), F32),
                            pltpu.SemaphoreType.DMA((2,))]),
        out_shape=jax.ShapeDtypeStruct((batch, n_rows, FOX_HD), BF16),
        compiler_params=_cparams("arbitrary", "arbitrary"),
        name="fox_decode",
    )(page_table, q, gcol, knew, vnew, cache_kt, cache_vt, lp8)


def _out_proj_kernel(oa_ref, ob_ref, oc_ref, w_ref, x_ref, gt_ref, o_ref):
    na, nb = oa_ref.shape[1], ob_ref.shape[1]
    acc = _dot(oa_ref[...].astype(BF16), w_ref[0:na, :])
    acc = acc + _dot(ob_ref[...].astype(BF16), w_ref[na:na + nb, :])
    acc = acc + _dot(oc_ref[...].astype(BF16), w_ref[na + nb:, :])
    o_ref[...] = x_ref[...] + (1.0 + gt_ref[...]) * acc


def _out_proj(oa, ob, oc, w, x, gt, seq_len):
    r, d = x.shape
    tm = ROW_TILE
    return pl.pallas_call(
        _out_proj_kernel,
        grid=(r // tm,),
        in_specs=[pl.BlockSpec((tm, oa.shape[1]), lambda i: (i, 0)),
                  pl.BlockSpec((tm, ob.shape[1]), lambda i: (i, 0)),
                  pl.BlockSpec((tm, oc.shape[1]), lambda i: (i, 0)),
                  pl.BlockSpec(w.shape, lambda i: (0, 0)),
                  pl.BlockSpec((tm, d), lambda i: (i, 0)),
                  _mod_spec(tm, d, seq_len)],
        out_specs=pl.BlockSpec((tm, d), lambda i: (i, 0)),
        out_shape=jax.ShapeDtypeStruct((r, d), F32),
        compiler_params=_cparams("parallel"),
        name="out_proj",
    )(oa, ob, oc, w, x, gt)


def _ffn_kernel(*refs, n_expert, routed, final_norm):
    refs = list(refs)
    x_ref, g_ref, sc_ref, sh_ref, gt_ref = refs[:5]
    pos = 5
    if routed:
        rt_ref = refs[pos]
        pos += 1
    wg_ref, wu_ref, wd_ref = refs[pos:pos + 3]
    pos += 3
    if final_norm:
        gf_ref = refs[pos]
        pos += 1
    o_ref, h_ref, acc_ref = refs[pos:pos + 3]
    if routed:
        gates_ref = refs[pos + 3]
    e = pl.program_id(1)
    f = pl.program_id(2)
    tm = x_ref.shape[0]

    @pl.when((e == 0) & (f == 0))
    def _():
        h = _rms(x_ref[...]) * g_ref[...]
        h = h * (1.0 + sc_ref[...]) + sh_ref[...]
        h_ref[...] = h.astype(BF16)
        acc_ref[...] = jnp.zeros_like(acc_ref)
        if routed:
            lane = _iota2((tm, 128), 1)
            logits = jnp.where(lane < n_expert, _dot_hi(h, rt_ref[...]), NEG)
            m1 = jnp.max(logits, axis=-1, keepdims=True)
            i1 = jnp.min(jnp.where(logits == m1, lane, 128), axis=-1, keepdims=True)
            rest = jnp.where(lane == i1, NEG, logits)
            m2 = jnp.max(rest, axis=-1, keepdims=True)
            i2 = jnp.min(jnp.where(rest == m2, lane, 128), axis=-1, keepdims=True)
            e2 = jnp.exp(m2 - m1)
            w1 = 1.0 / (1.0 + e2)
            gates_ref[...] = jnp.where(lane == i1, w1, 0.0) + jnp.where(lane == i2, e2 * w1, 0.0)

    hb = h_ref[...]
    a = _dot(hb, wg_ref[...])
    u = _dot(hb, wu_ref[...])
    y = _dot((_silu(a) * u).astype(BF16), wd_ref[...])
    if routed:
        lane = _iota2((tm, 128), 1)
        y = y * jnp.sum(jnp.where(lane == e, gates_ref[...], 0.0), axis=-1, keepdims=True)
    acc_ref[...] += y

    @pl.when((e == n_expert - 1) & (f == pl.num_programs(2) - 1))
    def _():
        out = x_ref[...] + (1.0 + gt_ref[...]) * acc_ref[...]
        if final_norm:
            out = _rms(out) * gf_ref[...]
        o_ref[...] = out


def _ffn(x, g, sc, sh, gt, wg, wu, wd, seq_len, router=None, g_final=None):
    r, d = x.shape
    n_expert, _, d_ff = wg.shape
    tm = ROW_TILE
    tf = d_ff // 2
    routed = router is not None
    final_norm = g_final is not None
    ms = _mod_spec(tm, d, seq_len)
    in_specs = [pl.BlockSpec((tm, d), lambda i, e, f: (i, 0)),
                pl.BlockSpec((1, d), lambda i, e, f: (0, 0)), ms, ms, ms]
    args = [x, g, sc, sh, gt]
    if routed:
        in_specs.append(pl.BlockSpec(router.shape, lambda i, e, f: (0, 0)))
        args.append(router)
    in_specs += [pl.BlockSpec((None, d, tf), lambda i, e, f: (e, 0, f)),
                 pl.BlockSpec((None, d, tf), lambda i, e, f: (e, 0, f)),
                 pl.BlockSpec((None, tf, d), lambda i, e, f: (e, f, 0))]
    args += [wg, wu, wd]
    if final_norm:
        in_specs.append(pl.BlockSpec((1, d), lambda i, e, f: (0, 0)))
        args.append(g_final)
    scratch = [pltpu.VMEM((tm, d), BF16), pltpu.VMEM((tm, d), F32)]
    if routed:
        scratch.append(pltpu.VMEM((tm, 128), F32))
    kern = functools.partial(_ffn_kernel, n_expert=n_expert, routed=routed, final_norm=final_norm)
    return pl.pallas_call(
        kern,
        grid=(r // tm, n_expert, d_ff // tf),
        in_specs=in_specs,
        out_specs=pl.BlockSpec((tm, d), lambda i, e, f: (i, 0)),
        out_shape=jax.ShapeDtypeStruct((r, d), F32),
        scratch_shapes=scratch,
        compiler_params=_cparams("parallel", "arbitrary", "arbitrary"),
        name="moe_ffn" if routed else "dense_ffn",
    )(*args)


def _rot_cols(w):
    half = w.shape[-1] // 2
    return jnp.concatenate([-w[..., half:], w[..., :half]], axis=-1)


def _prep_w_in(w):
    d = w.shape[0]
    sizes = (GDN_QKV, GDN_H * GDN_DV, GDN_H, GDN_H, MLA_QR, MLA_KVR, MLA_ROPE,
             FOX_H * FOX_HD, FOX_H * FOX_HD, FOX_H * FOX_HD, FOX_H)
    pieces, off = [], 0
    for s in sizes:
        pieces.append(w[:, off:off + s])
        off += s
    qkv, z, b, a, cq, ckv, kpe, qc, kc, vc, f = pieces
    used = 2 * MLA_ROPE + 2 * GDN_H + FOX_H
    small = jnp.concatenate([kpe, _rot_cols(kpe), b, a, f, jnp.zeros((d, 128 - used), w.dtype)], axis=1)
    return jnp.concatenate([qkv, z, cq, qc, kc, vc, ckv, small], axis=1).astype(BF16)


def _prep_w_qb(w):
    per = MLA_NOPE + MLA_ROPE
    cols = []
    for h in range(MLA_H):
        nope = w[:, h * per:h * per + MLA_NOPE]
        pe = w[:, h * per + MLA_NOPE:(h + 1) * per]
        cols += [nope, pe, _rot_cols(pe)]
    return jnp.concatenate(cols, axis=1).astype(BF16)


def _lane_row(values, offset):
    row = jnp.zeros((1, 128), F32)
    return row.at[0, offset:offset + values.shape[0]].set(values.astype(F32))


def _rope_table(pos):
    half = MLA_ROPE // 2
    inv = jnp.exp(-math.log(ROPE_BASE) * jnp.arange(half, dtype=F32) / half)
    ang = pos.astype(F32)[:, None] * inv[None, :]
    cos, sin = jnp.cos(ang), jnp.sin(ang)
    return jnp.concatenate([cos, cos, sin, sin], axis=-1)


def kernel(x_prompt, x_sample, cache_mla_latent, cache_mla_rope, cache_fox_k, cache_fox_v, cache_fox_logf, state_gdn, state_gdn_conv, page_table, c_prompt, c_sample, w_ada, b_ada, g_norm1, g_norm2, w_in, gdn_conv_w, gdn_a_log, gdn_dt_bias, gdn_norm_g, mla_q_norm_g, mla_w_qb, mla_kv_norm_g, mla_w_kvb, fox_f_bias, w_out, ffn_w_gate, ffn_w_up, ffn_w_down, moe_router, moe_w_gate, moe_w_up, moe_w_down, g_final):
    n_layer = w_in.shape[0]
    bp, tp, d = x_prompt.shape
    bs, ts, _ = x_sample.shape
    n_pages, page = page_table.shape[1], cache_mla_latent.shape[2]
    past_len = n_pages * page
    assert tp % ROW_TILE == 0 and (bs * ts) % ROW_TILE == 0 and ROW_TILE % ts == 0
    assert tp >= GDN_CONV - 1 and ts >= GDN_CONV - 1 and n_pages % PAGES_PER_STEP == 0

    n_c = bp + bs
    c_all = jnp.concatenate([c_prompt, c_sample, jnp.zeros((-n_c % 8, d), F32)], axis=0)
    mod = _adaln(c_all, w_ada, b_ada)

    lp = jnp.swapaxes(cache_fox_logf, 2, 3)
    lp8 = jnp.concatenate([lp, lp], axis=2)
    pool = cache_fox_k.shape[1]
    fox_kt = jnp.transpose(cache_fox_k, (0, 1, 3, 4, 2)).reshape(n_layer, pool, FOX_H * FOX_HD, page)
    fox_vt = jnp.transpose(cache_fox_v, (0, 1, 3, 4, 2)).reshape(n_layer, pool, FOX_H * FOX_HD, page)
    mla_pe_t = jnp.swapaxes(cache_mla_rope, 2, 3)

    cs_p = _rope_table(jnp.arange(tp))
    cs_s = jnp.tile(_rope_table(past_len + jnp.arange(ts)), (bs, 1))
    head_mask = (jnp.arange(FOX_H)[:, None] == (jnp.arange(FOX_H * FOX_HD) // FOX_HD)[None, :])

    xp = x_prompt.reshape(bp * tp, d)
    xs = x_sample.reshape(bs * ts, d)
    outs = {k: [] for k in ("lat_p", "lat_s", "pe_p", "pe_s", "fk_p", "fk_s", "fv_p", "fv_s",
                            "lf_p", "lf_s", "st_p", "st_s", "cv_p", "cv_s")}

    for l in range(n_layer):
        w_in_l = _prep_w_in(w_in[l])
        w_qb_l = _prep_w_qb(mla_w_qb[l])
        w_kvb = mla_w_kvb[l].reshape(MLA_KVR, MLA_H, MLA_NOPE + MLA_V)
        w_kb_t = jnp.transpose(w_kvb[..., :MLA_NOPE], (1, 2, 0)).astype(BF16)
        w_vb_h = jnp.transpose(w_kvb[..., MLA_NOPE:], (1, 0, 2)).astype(BF16)
        w_vb_all = w_kvb[..., MLA_NOPE:].reshape(MLA_KVR, MLA_H * MLA_V).astype(BF16)
        w_out_l = w_out[l].astype(BF16)
        alog_row = _lane_row(gdn_a_log[l], SM_A)
        dtb_row = _lane_row(gdn_dt_bias[l], SM_A)
        fb_row = _lane_row(fox_f_bias[l], SM_F)
        gn = gdn_norm_g[l].reshape(1, GDN_DV)
        gq = mla_q_norm_g[l].reshape(1, MLA_QR)
        gkv = mla_kv_norm_g[l].reshape(1, MLA_KVR)
        g1 = g_norm1[l].reshape(1, d)
        g2 = g_norm2[l].reshape(1, d)
        cw = gdn_conv_w[l]
        i_ffn = l // 2
        last = l == n_layer - 1

        mods_p = [m.reshape(bp, 1, d) for m in jnp.split(mod[l, :bp], 6, axis=-1)]
        mods_s = [jnp.repeat(m, ts, axis=0) for m in jnp.split(mod[l, bp:bp + bs], 6, axis=-1)]

        def mixer_tail(x, mods, oa, ob, oc, seq_len):
            x = _out_proj(oa, ob, oc, w_out_l, x, mods[2], seq_len)
            gf = g_final.reshape(1, d) if last else None
            if l % 2 == 0:
                return _ffn(x, g2, mods[4], mods[3], mods[5], ffn_w_gate[i_ffn][None].astype(BF16),
                            ffn_w_up[i_ffn][None].astype(BF16), ffn_w_down[i_ffn][None].astype(BF16),
                            seq_len, g_final=gf)
            router = jnp.pad(moe_router[i_ffn], ((0, 0), (0, 128 - moe_router.shape[-1])))
            return _ffn(x, g2, mods[4], mods[3], mods[5], moe_w_gate[i_ffn].astype(BF16),
                        moe_w_up[i_ffn].astype(BF16), moe_w_down[i_ffn].astype(BF16),
                        seq_len, router=router, g_final=gf)

        y = _in_proj(xp, g1, mods_p[1], mods_p[0], w_in_l, tp)
        oa, st = _gdn_prompt(y, cw, alog_row, dtb_row, gn, bp, tp)
        qcat, kcat, lat, kpe = _mla_prep(y, cs_p, gq, gkv, w_qb_l, w_kb_t)
        ob = _flash(qcat, kcat, kcat, bp, tp, shared_kv=True, wp=w_vb_h, d_v=MLA_KVR, d_out=MLA_V)
        fq, fk, fv, fx = _fox_prep(y, fb_row, tp, True)
        g_cum = fx[:, 128 + SM_F:128 + SM_F + FOX_H]
        oc = _flash(fq, fk, fv, bp, tp, shared_kv=False, gq=g_cum.T[:, :, None], gk=g_cum.T[:, None, :],
                    d_v=FOX_HD, d_out=FOX_HD)
        y3 = y.reshape(bp, tp, N_Y)
        outs["lat_p"].append(lat.reshape(bp, tp, MLA_KVR))
        outs["pe_p"].append(kpe.reshape(bp, tp, MLA_ROPE))
        outs["fk_p"].append(y3[:, :, C_KC:C_KC + FOX_H * FOX_HD].reshape(bp, tp, FOX_H, FOX_HD))
        outs["fv_p"].append(y3[:, :, C_VC:C_VC + FOX_H * FOX_HD].reshape(bp, tp, FOX_H, FOX_HD))
        outs["lf_p"].append(fx[:, SM_F:SM_F + FOX_H].reshape(bp, tp, FOX_H))
        outs["st_p"].append(st)
        outs["cv_p"].append(y3[:, tp - (GDN_CONV - 1):, C_QKV:C_QKV + GDN_QKV])
        xp = mixer_tail(xp, mods_p, oa, ob, oc, tp)

        y = _in_proj(xs, g1, mods_s[1], mods_s[0], w_in_l, ts)
        y3 = y.reshape(bs, ts, N_Y)
        oa, st = _gdn_decode(y3, state_gdn_conv[l], state_gdn[l], cw, alog_row, dtb_row, gn)
        qcat, kcat, lat, kpe = _mla_prep(y, cs_s, gq, gkv, w_qb_l, w_kb_t)
        q_rows = jnp.transpose(qcat.reshape(MLA_H, bs, ts, 256), (1, 2, 0, 3)).reshape(bs, ts * MLA_H, 256)
        knew = jnp.pad(kcat.reshape(bs, ts, 256), ((0, 0), (0, 128 - ts), (0, 0)))
        ob = _mla_decode(page_table, q_rows, knew, w_vb_all, cache_mla_latent, mla_pe_t, l, ts)
        (fx,) = _fox_prep(y, fb_row, ts, False)
        g_rows = fx[:, 128 + SM_F:128 + SM_F + FOX_H].reshape(bs, ts * FOX_H, 1)
        qc3 = y3[:, :, C_QC:C_QC + FOX_H * FOX_HD]
        q_bd = jnp.where(head_mask[None, None], qc3[:, :, None, :], 0.0).reshape(bs, ts * FOX_H, FOX_H * FOX_HD)
        kc3 = y3[:, :, C_KC:C_KC + FOX_H * FOX_HD]
        vc3 = y3[:, :, C_VC:C_VC + FOX_H * FOX_HD]
        pad_new = ((0, 0), (0, 128 - ts), (0, 0))
        oc = _fox_decode(page_table, q_bd.astype(BF16), g_rows, jnp.pad(kc3, pad_new).astype(BF16),
                         jnp.pad(vc3, pad_new).astype(BF16), fox_kt, fox_vt, lp8, l, ts)
        outs["lat_s"].append(lat.reshape(bs, ts, MLA_KVR))
        outs["pe_s"].append(kpe.reshape(bs, ts, MLA_ROPE))
        outs["fk_s"].append(kc3.reshape(bs, ts, FOX_H, FOX_HD))
        outs["fv_s"].append(vc3.reshape(bs, ts, FOX_H, FOX_HD))
        outs["lf_s"].append(fx[:, SM_F:SM_F + FOX_H].reshape(bs, ts, FOX_H))
        outs["st_s"].append(st)
        outs["cv_s"].append(y3[:, ts - (GDN_CONV - 1):, C_QKV:C_QKV + GDN_QKV])
        xs = mixer_tail(xs, mods_s, oa.reshape(bs * ts, GDN_H * GDN_DV), ob.reshape(bs * ts, MLA_H * MLA_V),
                        oc.reshape(bs * ts, FOX_H * FOX_HD), ts)

    stack = lambda k: jnp.stack(outs[k], axis=0)
    return (xp.reshape(bp, tp, d), xs.reshape(bs, ts, d),
            stack("lat_p"), stack("lat_s"), stack("pe_p"), stack("pe_s"),
            stack("fk_p"), stack("fk_s"), stack("fv_p"), stack("fv_s"),
            stack("lf_p"), stack("lf_s"), stack("st_p"), stack("st_s"),
            stack("cv_p"), stack("cv_s"))
```

```python
import functools
import math

import jax
import jax.numpy as jnp
from jax import lax
from jax.experimental import pallas as pl
from jax.experimental.pallas import tpu as pltpu

F32 = jnp.float32
BF16 = jnp.bfloat16
EPS = 1e-6
NEG = -1e30

GDN_H, GDN_DK, GDN_DV, GDN_CONV = 4, 128, 128, 4
GDN_QKV = GDN_H * (2 * GDN_DK + GDN_DV)
MLA_H, MLA_QR, MLA_KVR, MLA_NOPE, MLA_ROPE, MLA_V = 4, 256, 128, 64, 32, 64
FOX_H, FOX_HD = 4, 64
ROPE_BASE = 10000.0
MLA_SCALE = (MLA_NOPE + MLA_ROPE) ** -0.5
FOX_SCALE = FOX_HD ** -0.5

C_QKV, C_Z, C_CQ, C_QC, C_KC, C_VC, C_CKV, C_SM, N_Y = 0, 1536, 2048, 2304, 2560, 2816, 3072, 3200, 3328
SM_KPE, SM_KROT, SM_B, SM_A, SM_F = 0, 32, 64, 68, 72

V7X_VMEM_LIMIT = 52 * 1024 * 1024
ROW_TILE = 512


def _cparams(*sem):
    return pltpu.CompilerParams(dimension_semantics=sem, vmem_limit_bytes=V7X_VMEM_LIMIT)


def _dot(a, b):
    return jnp.dot(a, b, preferred_element_type=F32)


def _dot_nt(a, b):
    return lax.dot_general(a, b, (((1,), (1,)), ((), ())), preferred_element_type=F32)


def _split2(a):
    hi = a.astype(BF16)
    return hi, (a - hi.astype(F32)).astype(BF16)


def _split3(a):
    hi = a.astype(BF16)
    r = a - hi.astype(F32)
    mid = r.astype(BF16)
    return hi, mid, (r - mid.astype(F32)).astype(BF16)


def _dot_hi(a, b):
    ah, al = _split2(a)
    bh, bl = _split2(b)
    return _dot(ah, bh) + _dot(ah, bl) + _dot(al, bh)


def _dot_ones_lhs(ones_bf16, x):
    h, m, l = _split3(x)
    return _dot(ones_bf16, h) + _dot(ones_bf16, m) + _dot(ones_bf16, l)


def _dot_ones_rhs(x, ones_bf16):
    h, m, l = _split3(x)
    return _dot(h, ones_bf16) + _dot(m, ones_bf16) + _dot(l, ones_bf16)


def _sigmoid(x):
    return 1.0 / (1.0 + jnp.exp(-x))


def _silu(x):
    return x * _sigmoid(x)


def _softplus(x):
    return jnp.maximum(x, 0.0) + jnp.log(1.0 + jnp.exp(-jnp.abs(x)))


def _log_sigmoid(x):
    return -_softplus(-x)


def _rms(x):
    return x * lax.rsqrt(jnp.mean(x * x, axis=-1, keepdims=True) + EPS)


def _iota2(shape, dim):
    return lax.broadcasted_iota(jnp.int32, shape, dim)


def _ada_kernel(c_ref, w_ref, b_ref, o_ref):
    a = _silu(c_ref[...]).astype(BF16)
    o_ref[...] = _dot(a, w_ref[...].astype(BF16)) + b_ref[...]


def _adaln(c_all, w_ada, b_ada):
    n_layer, d, n = w_ada.shape
    r = c_all.shape[0]
    tn = 1536
    return pl.pallas_call(
        _ada_kernel,
        grid=(n_layer, n // tn),
        in_specs=[pl.BlockSpec((r, d), lambda l, j: (0, 0)),
                  pl.BlockSpec((None, d, tn), lambda l, j: (l, 0, j)),
                  pl.BlockSpec((None, 1, tn), lambda l, j: (l, 0, j))],
        out_specs=pl.BlockSpec((None, r, tn), lambda l, j: (l, 0, j)),
        out_shape=jax.ShapeDtypeStruct((n_layer, r, n), F32),
        compiler_params=_cparams("arbitrary", "arbitrary"),
        name="adaln",
    )(c_all, w_ada, b_ada.reshape(n_layer, 1, n))


def _mod_spec(tm, d, seq_len):
    if seq_len >= tm:
        return pl.BlockSpec((None, 1, d), lambda i, *_: ((i * tm) // seq_len, 0, 0))
    return pl.BlockSpec((tm, d), lambda i, *_: (i, 0))


def _in_proj_kernel(x_ref, g_ref, sc_ref, sh_ref, w_ref, o_ref):
    h = _rms(x_ref[...]) * g_ref[...]
    h = h * (1.0 + sc_ref[...]) + sh_ref[...]
    o_ref[...] = _dot(h.astype(BF16), w_ref[...])


def _in_proj(x, g, sc, sh, w, seq_len):
    r, d = x.shape
    n = w.shape[1]
    tm = 256
    ms = _mod_spec(tm, d, seq_len)
    return pl.pallas_call(
        _in_proj_kernel,
        grid=(r // tm,),
        in_specs=[pl.BlockSpec((tm, d), lambda i: (i, 0)),
                  pl.BlockSpec((1, d), lambda i: (0, 0)), ms, ms,
                  pl.BlockSpec((d, n), lambda i: (0, 0))],
        out_specs=pl.BlockSpec((tm, n), lambda i: (i, 0)),
        out_shape=jax.ShapeDtypeStruct((r, n), F32),
        compiler_params=_cparams("parallel"),
        name="in_proj",
    )(x, g, sc, sh, w)


def _gdn_prompt_kernel(qkv_ref, z_ref, sm_ref, cw_ref, alog_ref, dtb_ref, gn_ref, o_ref, so_ref,
                       ext_ref, s_ref, *, chunk, n_chunk, n_seq):
    c = pl.program_id(1)

    @pl.when(c == 0)
    def _():
        ext_ref[:, 0:8, :] = jnp.zeros((n_seq, 8, GDN_QKV), F32)
        s_ref[...] = jnp.zeros_like(s_ref)

    for i in range(n_seq):
        _gdn_chunk(qkv_ref.at[i], z_ref.at[i], sm_ref.at[i], cw_ref, alog_ref, dtb_ref, gn_ref,
                   o_ref.at[i], ext_ref.at[i], s_ref.at[i], chunk)

    @pl.when(c == n_chunk - 1)
    def _():
        so_ref[...] = s_ref[...]


def _gdn_chunk(qkv_ref, z_ref, sm_ref, cw_ref, alog_ref, dtb_ref, gn_ref, o_ref, ext_ref, s_ref, chunk):
    n_h, dk = GDN_H, GDN_DK
    ext_ref[8:8 + chunk, :] = qkv_ref[...]
    cw = cw_ref[...]
    conv = cw[0:1, :] * ext_ref[5:5 + chunk, :]
    for j in range(1, GDN_CONV):
        conv = conv + cw[j:j + 1, :] * ext_ref[5 + j:5 + j + chunk, :]
    ext_ref[0:8, :] = ext_ref[chunk:chunk + 8, :]
    act = _silu(conv)

    sm = sm_ref[...]
    g_log = -jnp.exp(alog_ref[...]) * _softplus(sm + dtb_ref[...])
    beta_all = _sigmoid(sm)
    row = _iota2((chunk, chunk), 0)
    col = _iota2((chunk, chunk), 1)
    incl = row >= col
    strict = row > col
    tri = jnp.where(incl, 1.0, 0.0).astype(BF16)
    eye = jnp.where(row == col, 1.0, 0.0)
    gc_all = _dot_ones_lhs(tri, g_log)
    gc_t = gc_all.T

    for h in range(n_h):
        q = act[:, h * dk:(h + 1) * dk]
        k = act[:, (n_h + h) * dk:(n_h + h + 1) * dk]
        v = act[:, 2 * n_h * dk + h * GDN_DV:2 * n_h * dk + (h + 1) * GDN_DV]
        q = q * lax.rsqrt(jnp.sum(q * q, axis=-1, keepdims=True) + EPS) * (dk ** -0.5)
        k = k * lax.rsqrt(jnp.sum(k * k, axis=-1, keepdims=True) + EPS)
        beta = beta_all[:, SM_B + h:SM_B + h + 1]
        gcol = gc_all[:, SM_A + h:SM_A + h + 1]
        grow = gc_t[SM_A + h:SM_A + h + 1, :]
        decay = jnp.where(incl, jnp.exp(jnp.where(incl, gcol - grow, 0.0)), 0.0)
        kb = k * beta
        vb = v * beta
        kbf = k.astype(BF16)
        x = -jnp.where(strict, _dot_nt(kb.astype(BF16), kbf) * decay, 0.0)
        minv = eye + x
        p = x
        for _ in range(int(math.log2(chunk)) - 1):
            p = _dot_hi(p, p)
            minv = minv + _dot_hi(minv, p)
        u = _dot_hi(minv, vb)
        w = _dot_hi(minv, kb * jnp.exp(gcol))
        aqk = _dot_nt(q.astype(BF16), kbf) * decay
        s_h = s_ref[h]
        s_bf = s_h.astype(BF16)
        v_new = u - _dot(w.astype(BF16), s_bf)
        o = _dot((q * jnp.exp(gcol)).astype(BF16), s_bf) + _dot(aqk.astype(BF16), v_new.astype(BF16))
        g_last = gcol[chunk - 1:chunk, :]
        k_dec_t = (k * jnp.exp(g_last - gcol)).T
        s_ref[h] = s_h * jnp.exp(g_last) + _dot(k_dec_t.astype(BF16), v_new.astype(BF16))
        o = _rms(o) * gn_ref[...] * _silu(z_ref[:, h * GDN_DV:(h + 1) * GDN_DV])
        o_ref[:, h * GDN_DV:(h + 1) * GDN_DV] = o.astype(o_ref.dtype)


def _gdn_prompt(y, cw, alog_row, dtb_row, gn, batch, seq_len):
    chunk = 128
    n_chunk = seq_len // chunk
    n_seq = 2 if batch % 2 == 0 else 1
    y3 = y.reshape(batch, seq_len, y.shape[1])
    kern = functools.partial(_gdn_prompt_kernel, chunk=chunk, n_chunk=n_chunk, n_seq=n_seq)
    o, st = pl.pallas_call(
        kern,
        grid=(batch // n_seq, n_chunk),
        in_specs=[pl.BlockSpec((n_seq, chunk, GDN_QKV), lambda b, c: (b, c, C_QKV // GDN_QKV)),
                  pl.BlockSpec((n_seq, chunk, 512), lambda b, c: (b, c, C_Z // 512)),
                  pl.BlockSpec((n_seq, chunk, 128), lambda b, c: (b, c, C_SM // 128)),
                  pl.BlockSpec((GDN_CONV, GDN_QKV), lambda b, c: (0, 0)),
                  pl.BlockSpec((1, 128), lambda b, c: (0, 0)),
                  pl.BlockSpec((1, 128), lambda b, c: (0, 0)),
                  pl.BlockSpec((1, GDN_DV), lambda b, c: (0, 0))],
        out_specs=[pl.BlockSpec((n_seq, chunk, GDN_H * GDN_DV), lambda b, c: (b, c, 0)),
                   pl.BlockSpec((n_seq, GDN_H, GDN_DK, GDN_DV), lambda b, c: (b, 0, 0, 0))],
        out_shape=[jax.ShapeDtypeStruct((batch, seq_len, GDN_H * GDN_DV), BF16),
                   jax.ShapeDtypeStruct((batch, GDN_H, GDN_DK, GDN_DV), F32)],
        scratch_shapes=[pltpu.VMEM((n_seq, chunk + 8, GDN_QKV), F32),
                        pltpu.VMEM((n_seq, GDN_H, GDN_DK, GDN_DV), F32)],
        compiler_params=_cparams("arbitrary", "arbitrary"),
        name="gdn_prompt",
    )(y3, y3, y3, cw, alog_row, dtb_row, gn)
    return o.reshape(batch * seq_len, GDN_H * GDN_DV), st


def _gdn_decode_kernel(qkv_ref, z_ref, sm_ref, buf_ref, s0_ref, cw_ref, alog_ref, dtb_ref, gn_ref,
                       o_ref, so_ref, ext_ref, qk_ref, *, n_tok):
    n_h, dk = GDN_H, GDN_DK
    ext_ref[0:GDN_CONV - 1, :] = buf_ref[...]
    ext_ref[GDN_CONV - 1:GDN_CONV - 1 + n_tok, :] = qkv_ref[...]
    cw = cw_ref[...]
    conv = cw[0:1, :] * ext_ref[0:n_tok, :]
    for j in range(1, GDN_CONV):
        conv = conv + cw[j:j + 1, :] * ext_ref[j:j + n_tok, :]
    act = _silu(conv)
    sm = sm_ref[...]
    decay_all = jnp.exp(-jnp.exp(alog_ref[...]) * _softplus(sm + dtb_ref[...]))
    beta_all = _sigmoid(sm)

    qk_ref[...] = jnp.zeros_like(qk_ref)
    for h in range(n_h):
        q = act[:, h * dk:(h + 1) * dk]
        k = act[:, (n_h + h) * dk:(n_h + h + 1) * dk]
        q = q * lax.rsqrt(jnp.sum(q * q, axis=-1, keepdims=True) + EPS) * (dk ** -0.5)
        k = k * lax.rsqrt(jnp.sum(k * k, axis=-1, keepdims=True) + EPS)
        qk_ref[h * n_tok:(h + 1) * n_tok, :] = q
        qk_ref[(n_h + h) * n_tok:(n_h + h + 1) * n_tok, :] = k
    qk_t = qk_ref[...].T

    for h in range(n_h):
        s = s0_ref[h]
        for t in range(n_tok):
            q_col = qk_t[:, h * n_tok + t:h * n_tok + t + 1]
            k_col = qk_t[:, (n_h + h) * n_tok + t:(n_h + h) * n_tok + t + 1]
            v_row = act[t:t + 1, 2 * n_h * dk + h * GDN_DV:2 * n_h * dk + (h + 1) * GDN_DV]
            s = s * decay_all[t:t + 1, SM_A + h:SM_A + h + 1]
            pred = jnp.sum(k_col * s, axis=0, keepdims=True)
            v_new = beta_all[t:t + 1, SM_B + h:SM_B + h + 1] * (v_row - pred)
            s = s + k_col * v_new
            o = jnp.sum(q_col * s, axis=0, keepdims=True)
            o = _rms(o) * gn_ref[...] * _silu(z_ref[t:t + 1, h * GDN_DV:(h + 1) * GDN_DV])
            o_ref[t:t + 1, h * GDN_DV:(h + 1) * GDN_DV] = o.astype(o_ref.dtype)
        so_ref[h] = s


def _gdn_decode(y3, buf, s0, cw, alog_row, dtb_row, gn):
    batch, n_tok, _ = y3.shape
    kern = functools.partial(_gdn_decode_kernel, n_tok=n_tok)
    return pl.pallas_call(
        kern,
        grid=(batch,),
        in_specs=[pl.BlockSpec((None, n_tok, GDN_QKV), lambda b: (b, 0, C_QKV // GDN_QKV)),
                  pl.BlockSpec((None, n_tok, 512), lambda b: (b, 0, C_Z // 512)),
                  pl.BlockSpec((None, n_tok, 128), lambda b: (b, 0, C_SM // 128)),
                  pl.BlockSpec((None, GDN_CONV - 1, GDN_QKV), lambda b: (b, 0, 0)),
                  pl.BlockSpec((None, GDN_H, GDN_DK, GDN_DV), lambda b: (b, 0, 0, 0)),
                  pl.BlockSpec((GDN_CONV, GDN_QKV), lambda b: (0, 0)),
                  pl.BlockSpec((1, 128), lambda b: (0, 0)),
                  pl.BlockSpec((1, 128), lambda b: (0, 0)),
                  pl.BlockSpec((1, GDN_DV), lambda b: (0, 0))],
        out_specs=[pl.BlockSpec((None, n_tok, GDN_H * GDN_DV), lambda b: (b, 0, 0)),
                   pl.BlockSpec((None, GDN_H, GDN_DK, GDN_DV), lambda b: (b, 0, 0, 0))],
        out_shape=[jax.ShapeDtypeStruct((batch, n_tok, GDN_H * GDN_DV), F32),
                   jax.ShapeDtypeStruct((batch, GDN_H, GDN_DK, GDN_DV), F32)],
        scratch_shapes=[pltpu.VMEM((16, GDN_QKV), F32), pltpu.VMEM((128, GDN_DK), F32)],
        compiler_params=_cparams("arbitrary"),
        name="gdn_decode",
    )(y3, y3, y3, buf, s0, cw, alog_row, dtb_row, gn)


def _mla_prep_kernel(cq_ref, ckv_ref, sm_ref, cs_ref, gq_ref, gkv_ref, wq_ref, wkb_ref,
                     qcat_ref, kcat_ref, lat_ref, kpe_ref):
    tm = cq_ref.shape[0]
    cs = cs_ref[...]
    cos = cs[:, 0:MLA_ROPE]
    sin = cs[:, MLA_ROPE:2 * MLA_ROPE]
    lat = _rms(ckv_ref[...]) * gkv_ref[...]
    sm = sm_ref[...]
    kpe = sm[:, SM_KPE:SM_KPE + MLA_ROPE] * cos + sm[:, SM_KROT:SM_KROT + MLA_ROPE] * sin
    lat_ref[...] = lat
    kpe_ref[...] = kpe
    pad = jnp.zeros((tm, 256 - MLA_KVR - MLA_ROPE), F32)
    kcat_ref[:, 0:MLA_KVR] = lat.astype(kcat_ref.dtype)
    kcat_ref[:, MLA_KVR:MLA_KVR + MLA_ROPE] = kpe.astype(kcat_ref.dtype)
    kcat_ref[:, MLA_KVR + MLA_ROPE:256] = pad.astype(kcat_ref.dtype)

    qn = _rms(cq_ref[...]) * gq_ref[...]
    qb = _dot(qn.astype(BF16), wq_ref[...])
    for h in range(MLA_H):
        base = h * 128
        nope = qb[:, base:base + MLA_NOPE]
        pe = (qb[:, base + MLA_NOPE:base + MLA_NOPE + MLA_ROPE] * cos
              + qb[:, base + MLA_NOPE + MLA_ROPE:base + 128] * sin)
        q_abs = _dot(nope.astype(BF16), wkb_ref[h])
        qcat_ref[h, :, 0:MLA_KVR] = (q_abs * MLA_SCALE).astype(qcat_ref.dtype)
        qcat_ref[h, :, MLA_KVR:MLA_KVR + MLA_ROPE] = (pe * MLA_SCALE).astype(qcat_ref.dtype)
        qcat_ref[h, :, MLA_KVR + MLA_ROPE:256] = pad.astype(qcat_ref.dtype)


def _mla_prep(y, cs, gq, gkv, wq, wkb):
    r = y.shape[0]
    tm = ROW_TILE
    n_tab = cs.shape[0] // tm
    return pl.pallas_call(
        _mla_prep_kernel,
        grid=(r // tm,),
        in_specs=[pl.BlockSpec((tm, MLA_QR), lambda i: (i, C_CQ // MLA_QR)),
                  pl.BlockSpec((tm, MLA_KVR), lambda i: (i, C_CKV // MLA_KVR)),
                  pl.BlockSpec((tm, 128), lambda i: (i, C_SM // 128)),
                  pl.BlockSpec((tm, 2 * MLA_ROPE), lambda i: (i % n_tab, 0)),
                  pl.BlockSpec((1, MLA_QR), lambda i: (0, 0)),
                  pl.BlockSpec((1, MLA_KVR), lambda i: (0, 0)),
                  pl.BlockSpec((MLA_QR, MLA_H * 128), lambda i: (0, 0)),
                  pl.BlockSpec((MLA_H, MLA_NOPE, MLA_KVR), lambda i: (0, 0, 0))],
        out_specs=[pl.BlockSpec((MLA_H, tm, 256), lambda i: (0, i, 0)),
                   pl.BlockSpec((tm, 256), lambda i: (i, 0)),
                   pl.BlockSpec((tm, MLA_KVR), lambda i: (i, 0)),
                   pl.BlockSpec((tm, MLA_ROPE), lambda i: (i, 0))],
        out_shape=[jax.ShapeDtypeStruct((MLA_H, r, 256), BF16),
                   jax.ShapeDtypeStruct((r, 256), BF16),
                   jax.ShapeDtypeStruct((r, MLA_KVR), F32),
                   jax.ShapeDtypeStruct((r, MLA_ROPE), F32)],
        compiler_params=_cparams("parallel"),
        name="mla_prep",
    )(y, y, y, cs, gq, gkv, wq, wkb)


def _fox_prep_kernel(*refs, seq_len, tm, emit_heads):
    if emit_heads:
        qc_ref, kc_ref, vc_ref, sm_ref, fb_ref, q_ref, k_ref, v_ref, fx_ref, carry_ref = refs
    else:
        sm_ref, fb_ref, fx_ref, carry_ref = refs
    i = pl.program_id(0)
    logf = _log_sigmoid(sm_ref[...] + fb_ref[...])
    row = _iota2((tm, tm), 0)
    col = _iota2((tm, tm), 1)
    if seq_len >= tm:
        @pl.when((i * tm) % seq_len == 0)
        def _():
            carry_ref[...] = jnp.zeros_like(carry_ref)
        tri = jnp.where(row >= col, 1.0, 0.0).astype(BF16)
        g = _dot_ones_lhs(tri, logf) + carry_ref[...]
        carry_ref[...] = g[tm - 1:tm, :]
    else:
        same = (row // seq_len) == (col // seq_len)
        tri = jnp.where((row >= col) & same, 1.0, 0.0).astype(BF16)
        g = _dot_ones_lhs(tri, logf)
    fx_ref[:, 0:128] = logf
    fx_ref[:, 128:256] = g
    if emit_heads:
        for h in range(FOX_H):
            sl = slice(h * FOX_HD, (h + 1) * FOX_HD)
            q_ref[h] = (qc_ref[:, sl] * FOX_SCALE).astype(BF16)
            k_ref[h] = kc_ref[:, sl].astype(BF16)
            v_ref[h] = vc_ref[:, sl].astype(BF16)


def _fox_prep(y, fb_row, seq_len, emit_heads):
    r = y.shape[0]
    tm = ROW_TILE
    kern = functools.partial(_fox_prep_kernel, seq_len=seq_len, tm=tm, emit_heads=emit_heads)
    sm_spec = pl.BlockSpec((tm, 128), lambda i: (i, C_SM // 128))
    fb_spec = pl.BlockSpec((1, 128), lambda i: (0, 0))
    fx_spec = pl.BlockSpec((tm, 256), lambda i: (i, 0))
    fx_shape = jax.ShapeDtypeStruct((r, 256), F32)
    w = FOX_H * FOX_HD
    if emit_heads:
        head_spec = pl.BlockSpec((FOX_H, tm, FOX_HD), lambda i: (0, i, 0))
        head_shape = jax.ShapeDtypeStruct((FOX_H, r, FOX_HD), BF16)
        in_specs = [pl.BlockSpec((tm, w), lambda i: (i, C_QC // w)),
                    pl.BlockSpec((tm, w), lambda i: (i, C_KC // w)),
                    pl.BlockSpec((tm, w), lambda i: (i, C_VC // w)), sm_spec, fb_spec]
        args = (y, y, y, y, fb_row)
        out_specs = [head_spec, head_spec, head_spec, fx_spec]
        out_shape = [head_shape, head_shape, head_shape, fx_shape]
    else:
        in_specs = [sm_spec, fb_spec]
        args = (y, fb_row)
        out_specs = [fx_spec]
        out_shape = [fx_shape]
    return pl.pallas_call(
        kern, grid=(r // tm,), in_specs=in_specs, out_specs=out_specs, out_shape=out_shape,
        scratch_shapes=[pltpu.VMEM((1, 128), F32)],
        compiler_params=_cparams("arbitrary"),
        name="fox_prep",
    )(*args)


def _flash_kernel(qt_ref, kt_ref, *refs, n_head, shared_kv, has_bias, has_proj, d_v, d_out):
    refs = list(refs)
    q_ref, k_ref, v_ref = refs[:3]
    pos = 3
    if has_bias:
        gq_ref, gk_ref = refs[pos:pos + 2]
        pos += 2
    if has_proj:
        wp_ref = refs[pos]
        pos += 1
    o_ref, m_ref, l_ref, acc_ref = refs[pos:pos + 4]
    qi = qt_ref[pl.program_id(1)]
    ki = kt_ref[pl.program_id(1)]
    tq = q_ref.shape[1]
    tk = k_ref.shape[-2]

    @pl.when(ki == 0)
    def _():
        m_ref[...] = jnp.full(m_ref.shape, NEG, F32)
        l_ref[...] = jnp.zeros_like(l_ref)
        acc_ref[...] = jnp.zeros_like(acc_ref)

    def update(masked):
        for h in range(n_head):
            k = k_ref[...] if shared_kv else k_ref[h]
            v = v_ref[...] if shared_kv else v_ref[h]
            s = _dot_nt(q_ref[h], k)
            if has_bias:
                s = s + gq_ref[h] - gk_ref[h]
            if masked:
                s = jnp.where(_iota2((tq, tk), 1) <= _iota2((tq, tk), 0), s, NEG)
            m_prev = m_ref[h]
            m_new = jnp.maximum(m_prev, jnp.max(s, axis=-1, keepdims=True))
            alpha = jnp.exp(m_prev - m_new)
            p = jnp.exp(s - m_new)
            l_ref[h] = alpha * l_ref[h] + jnp.sum(p, axis=-1, keepdims=True)
            acc_ref[h] = alpha * acc_ref[h] + _dot(p.astype(BF16), v)
            m_ref[h] = m_new

    @pl.when(ki < qi)
    def _():
        update(False)

    @pl.when(ki == qi)
    def _():
        update(True)
        for h in range(n_head):
            o = acc_ref[h] / l_ref[h]
            if has_proj:
                o = _dot(o.astype(BF16), wp_ref[h])
            o_ref[:, h * d_out:(h + 1) * d_out] = o.astype(o_ref.dtype)


def _flash(q, k, v, batch, seq_len, *, shared_kv, gq=None, gk=None, wp=None, d_v, d_out):
    n_head, r, d_k = q.shape
    t = ROW_TILE
    nb = seq_len // t
    has_bias = gq is not None
    has_proj = wp is not None
    kern = functools.partial(_flash_kernel, n_head=n_head, shared_kv=shared_kv, has_bias=has_bias,
                             has_proj=has_proj, d_v=d_v, d_out=d_out)
    pairs = [(i, j) for i in range(nb) for j in range(i + 1)]
    q_tab = jnp.asarray([i for i, _ in pairs], jnp.int32)
    k_tab = jnp.asarray([j for _, j in pairs], jnp.int32)
    q_spec = pl.BlockSpec((n_head, t, d_k), lambda b, p, qt, kt: (0, b * nb + qt[p], 0))
    if shared_kv:
        k_spec = pl.BlockSpec((t, d_k), lambda b, p, qt, kt: (b * nb + kt[p], 0))
        v_spec = pl.BlockSpec((t, d_v), lambda b, p, qt, kt: (b * nb + kt[p], 0))
    else:
        k_spec = pl.BlockSpec((n_head, t, d_k), lambda b, p, qt, kt: (0, b * nb + kt[p], 0))
        v_spec = pl.BlockSpec((n_head, t, d_v), lambda b, p, qt, kt: (0, b * nb + kt[p], 0))
    in_specs = [q_spec, k_spec, v_spec]
    args = [q, k, v]
    if has_bias:
        in_specs += [pl.BlockSpec((n_head, t, 1), lambda b, p, qt, kt: (0, b * nb + qt[p], 0)),
                     pl.BlockSpec((n_head, 1, t), lambda b, p, qt, kt: (0, 0, b * nb + kt[p]))]
        args += [gq, gk]
    if has_proj:
        in_specs.append(pl.BlockSpec(wp.shape, lambda b, p, qt, kt: (0, 0, 0)))
        args.append(wp)
    return pl.pallas_call(
        kern,
        grid_spec=pltpu.PrefetchScalarGridSpec(
            num_scalar_prefetch=2, grid=(batch, len(pairs)), in_specs=in_specs,
            out_specs=pl.BlockSpec((t, n_head * d_out), lambda b, p, qt, kt: (b * nb + qt[p], 0)),
            scratch_shapes=[pltpu.VMEM((n_head, t, 1), F32), pltpu.VMEM((n_head, t, 1), F32),
                            pltpu.VMEM((n_head, t, d_v), F32)]),
        out_shape=jax.ShapeDtypeStruct((r, n_head * d_out), BF16),
        compiler_params=_cparams("parallel", "arbitrary"),
        name="flash_shared" if shared_kv else "flash_heads",
    )(q_tab, k_tab, *args)


PAGES_PER_STEP = 32


def _softmax_step(s, m_ref, l_ref):
    m_prev = m_ref[...]
    m_new = jnp.maximum(m_prev, jnp.max(s, axis=-1, keepdims=True))
    alpha = jnp.exp(m_prev - m_new)
    p = jnp.exp(s - m_new)
    l_ref[...] = alpha * l_ref[...] + jnp.sum(p, axis=-1, keepdims=True)
    m_ref[...] = m_new
    return alpha, p


def _new_token_mask(n_rows, n_keys, n_tok, n_head):
    t = _iota2((n_rows, n_keys), 0) // n_head
    j = _iota2((n_rows, n_keys), 1)
    return (j <= t) & (j < n_tok)


def _page_copies(pt_ref, b, c, slot, srcs, bufs, sem, *, layer, pps, n_pages, reverse, start):
    for j in range(pps):
        logical = c * pps + j
        pid = pt_ref[b, (n_pages - 1 - logical) if reverse else logical]
        for src, buf in zip(srcs, bufs):
            cp = pltpu.make_async_copy(src.at[layer, pid], buf.at[slot, j], sem.at[slot])
            if start:
                cp.start()
            else:
                cp.wait()


def _paged_fetch(fetch, n_step):
    b = pl.program_id(0)
    c = pl.program_id(1)
    g = b * n_step + c
    slot = g % 2

    @pl.when(g == 0)
    def _():
        fetch(b, c, slot, start=True)

    @pl.when(g + 1 < pl.num_programs(0) * n_step)
    def _():
        wrap = c + 1 == n_step
        fetch(jnp.where(wrap, b + 1, b), jnp.where(wrap, 0, c + 1), 1 - slot, start=True)

    fetch(b, c, slot, start=False)
    return slot


def _mla_decode_kernel(pt_ref, q_ref, knew_ref, wvb_ref, lat_hbm, pe_hbm, o_ref, m_ref, l_ref, acc_ref,
                       lat_buf, pe_buf, sem, *, layer, pps, n_step, n_tok):
    c = pl.program_id(1)
    n_rows = q_ref.shape[0]
    fetch = functools.partial(_page_copies, pt_ref, srcs=(lat_hbm, pe_hbm), bufs=(lat_buf, pe_buf), sem=sem,
                              layer=layer, pps=pps, n_pages=n_step * pps, reverse=False)
    slot = _paged_fetch(fetch, n_step)

    @pl.when(c == 0)
    def _():
        m_ref[...] = jnp.full(m_ref.shape, NEG, F32)
        l_ref[...] = jnp.zeros_like(l_ref)
        acc_ref[...] = jnp.zeros_like(acc_ref)

    q = q_ref[...]
    q_abs = q[:, 0:MLA_KVR]
    q_pe = q[:, MLA_KVR:MLA_KVR + MLA_ROPE]
    page = lat_buf.shape[2]
    lat = lat_buf[slot].reshape(pps * page, MLA_KVR).astype(BF16)
    pe_t = jnp.concatenate([pe_buf[slot, j].astype(BF16) for j in range(pps)], axis=1)
    s = _dot_nt(q_abs, lat) + _dot(q_pe, pe_t)
    alpha, p = _softmax_step(s, m_ref, l_ref)
    acc_ref[...] = alpha * acc_ref[...] + _dot(p.astype(BF16), lat)

    @pl.when(c == n_step - 1)
    def _():
        kn = knew_ref[...]
        s_new = _dot_nt(q, kn)
        s_new = jnp.where(_new_token_mask(n_rows, kn.shape[0], n_tok, MLA_H), s_new, NEG)
        alpha2, p2 = _softmax_step(s_new, m_ref, l_ref)
        o = (alpha2 * acc_ref[...] + _dot(p2.astype(BF16), kn[:, 0:MLA_KVR])) / l_ref[...]
        full = _dot(o.astype(BF16), wvb_ref[...])
        head = _iota2((n_rows, MLA_V), 0) % MLA_H
        out = jnp.zeros((n_rows, MLA_V), F32)
        for h in range(MLA_H):
            out = out + jnp.where(head == h, full[:, h * MLA_V:(h + 1) * MLA_V], 0.0)
        o_ref[...] = out.astype(o_ref.dtype)


def _mla_decode(page_table, q, knew, wvb, cache_lat, cache_pe_t, layer, n_tok):
    batch, n_rows, _ = q.shape
    n_pages = page_table.shape[1]
    pps = PAGES_PER_STEP
    n_step = n_pages // pps
    kern = functools.partial(_mla_decode_kernel, layer=layer, pps=pps, n_step=n_step, n_tok=n_tok)
    in_specs = [pl.BlockSpec((None, n_rows, 256), lambda b, c, pt: (b, 0, 0)),
                pl.BlockSpec((None, knew.shape[1], 256), lambda b, c, pt: (b, 0, 0)),
                pl.BlockSpec(wvb.shape, lambda b, c, pt: (0, 0)),
                pl.BlockSpec(memory_space=pl.ANY), pl.BlockSpec(memory_space=pl.ANY)]
    return pl.pallas_call(
        kern,
        grid_spec=pltpu.PrefetchScalarGridSpec(
            num_scalar_prefetch=1, grid=(batch, n_step), in_specs=in_specs,
            out_specs=pl.BlockSpec((None, n_rows, MLA_V), lambda b, c, pt: (b, 0, 0)),
            scratch_shapes=[pltpu.VMEM((n_rows, 1), F32), pltpu.VMEM((n_rows, 1), F32),
                            pltpu.VMEM((n_rows, MLA_KVR), F32),
                            pltpu.VMEM((2, pps) + cache_lat.shape[2:], F32),
                            pltpu.VMEM((2, pps) + cache_pe_t.shape[2:], F32),
                            pltpu.SemaphoreType.DMA((2,))]),
        out_shape=jax.ShapeDtypeStruct((batch, n_rows, MLA_V), BF16),
        compiler_params=_cparams("arbitrary", "arbitrary"),
        name="mla_decode",
    )(page_table, q, knew, wvb, cache_lat, cache_pe_t)


def _fox_decode_kernel(pt_ref, q_ref, g_ref, knew_ref, vnew_ref, k_hbm, v_hbm, lp_hbm, o_ref, m_ref, l_ref,
                       acc_ref, tot_ref, k_buf, v_buf, lp_buf, sem, *, layer, pps, n_step, n_tok):
    c = pl.program_id(1)
    n_rows = q_ref.shape[0]
    n_h, hd = FOX_H, FOX_HD
    page = lp_buf.shape[-1]
    fetch = functools.partial(_page_copies, pt_ref, srcs=(k_hbm, v_hbm, lp_hbm), bufs=(k_buf, v_buf, lp_buf),
                              sem=sem, layer=layer, pps=pps, n_pages=n_step * pps, reverse=True)
    slot = _paged_fetch(fetch, n_step)

    @pl.when(c == 0)
    def _():
        m_ref[...] = jnp.full(m_ref.shape, NEG, F32)
        l_ref[...] = jnp.zeros_like(l_ref)
        acc_ref[...] = jnp.zeros_like(acc_ref)
        tot_ref[...] = jnp.zeros_like(tot_ref)

    q = q_ref[...]
    later = jnp.where(_iota2((page, page), 0) >= _iota2((page, page), 1), 1.0, 0.0).astype(BF16)
    k_t = jnp.concatenate([k_buf[slot, j].astype(BF16) for j in range(pps)], axis=1)
    v_t = jnp.concatenate([v_buf[slot, j].astype(BF16) for j in range(pps)], axis=1)
    lp_all = lp_buf[slot].reshape(pps * 2 * n_h, page)
    rev_all = _dot_ones_rhs(lp_all, later)
    tot = tot_ref[...]
    parts = []
    for j in range(pps):
        rev = rev_all[j * 2 * n_h:(j + 1) * 2 * n_h, :]
        parts.append(tot + rev - lp_all[j * 2 * n_h:(j + 1) * 2 * n_h, :])
        tot = tot + rev[:, 0:1]
    tot_ref[...] = tot
    suffix = jnp.concatenate(parts, axis=1)
    s = (_dot(q, k_t) * FOX_SCALE + jnp.concatenate([suffix] * (n_rows // (2 * n_h)), axis=0) + g_ref[...])
    alpha, p = _softmax_step(s, m_ref, l_ref)
    acc_ref[...] = alpha * acc_ref[...] + _dot_nt(p.astype(BF16), v_t)

    @pl.when(c == n_step - 1)
    def _():
        kn = knew_ref[...]
        gcol = g_ref[...]
        s_new = _dot_nt(q, kn) * FOX_SCALE
        rows_i = _iota2((n_rows, kn.shape[0]), 0)
        cols_j = _iota2((n_rows, kn.shape[0]), 1)
        gk = jnp.zeros((n_rows, kn.shape[0]), F32)
        for j in range(n_tok):
            for h in range(n_h):
                gj = gcol[j * n_h + h:j * n_h + h + 1, :]
                gk = gk + jnp.where((cols_j == j) & (rows_i % n_h == h), gj, 0.0)
        s_new = s_new + gcol - gk
        s_new = jnp.where(_new_token_mask(n_rows, kn.shape[0], n_tok, n_h), s_new, NEG)
        alpha2, p2 = _softmax_step(s_new, m_ref, l_ref)
        full = alpha2 * acc_ref[...] + _dot(p2.astype(BF16), vnew_ref[...])
        head = _iota2((n_rows, hd), 0) % n_h
        out = jnp.zeros((n_rows, hd), F32)
        for h in range(n_h):
            out = out + jnp.where(head == h, full[:, h * hd:(h + 1) * hd], 0.0)
        o_ref[...] = (out / l_ref[...]).astype(o_ref.dtype)


def _fox_decode(page_table, q, gcol, knew, vnew, cache_kt, cache_vt, lp8, layer, n_tok):
    batch, n_rows, _ = q.shape
    n_pages = page_table.shape[1]
    pps = PAGES_PER_STEP
    n_step = n_pages // pps
    page = cache_kt.shape[3]
    kern = functools.partial(_fox_decode_kernel, layer=layer, pps=pps, n_step=n_step, n_tok=n_tok)
    w = FOX_H * FOX_HD
    hbm = pl.BlockSpec(memory_space=pl.ANY)
    in_specs = [pl.BlockSpec((None, n_rows, w), lambda b, c, pt: (b, 0, 0)),
                pl.BlockSpec((None, n_rows, 1), lambda b, c, pt: (b, 0, 0)),
                pl.BlockSpec((None, knew.shape[1], w), lambda b, c, pt: (b, 0, 0)),
                pl.BlockSpec((None, vnew.shape[1], w), lambda b, c, pt: (b, 0, 0)),
                hbm, hbm, hbm]
    return pl.pallas_call(
        kern,
        grid_spec=pltpu.PrefetchScalarGridSpec(
            num_scalar_prefetch=1, grid=(batch, n_step), in_specs=in_specs,
            out_specs=pl.BlockSpec((None, n_rows, FOX_HD), lambda b, c, pt: (b, 0, 0)),
            scratch_shapes=[pltpu.VMEM((n_rows, 1), F32), pltpu.VMEM((n_rows, 1), F32),
                            pltpu.VMEM((n_rows, FOX_H * FOX_HD), F32), pltpu.VMEM((2 * FOX_H, page), F32),
                            pltpu.VMEM((2, pps, w, page), F32), pltpu.VMEM((2, pps, w, page), F32),
                            pltpu.VMEM((2, pps, 2 * FOX_H, page), F32),
                            pltpu.SemaphoreType.DMA((2,))]),
        out_shape=jax.ShapeDtypeStruct((batch, n_rows, FOX_HD), BF16),
        compiler_params=_cparams("arbitrary", "arbitrary"),
        name="fox_decode",
    )(page_table, q, gcol, knew, vnew, cache_kt, cache_vt, lp8)


def _out_proj_kernel(oa_ref, ob_ref, oc_ref, w_ref, x_ref, gt_ref, o_ref):
    na, nb = oa_ref.shape[1], ob_ref.shape[1]
    acc = _dot(oa_ref[...].astype(BF16), w_ref[0:na, :])
    acc = acc + _dot(ob_ref[...].astype(BF16), w_ref[na:na + nb, :])
    acc = acc + _dot(oc_ref[...].astype(BF16), w_ref[na + nb:, :])
    o_ref[...] = x_ref[...] + (1.0 + gt_ref[...]) * acc


def _out_proj(oa, ob, oc, w, x, gt, seq_len):
    r, d = x.shape
    tm = ROW_TILE
    return pl.pallas_call(
        _out_proj_kernel,
        grid=(r // tm,),
        in_specs=[pl.BlockSpec((tm, oa.shape[1]), lambda i: (i, 0)),
                  pl.BlockSpec((tm, ob.shape[1]), lambda i: (i, 0)),
                  pl.BlockSpec((tm, oc.shape[1]), lambda i: (i, 0)),
                  pl.BlockSpec(w.shape, lambda i: (0, 0)),
                  pl.BlockSpec((tm, d), lambda i: (i, 0)),
                  _mod_spec(tm, d, seq_len)],
        out_specs=pl.BlockSpec((tm, d), lambda i: (i, 0)),
        out_shape=jax.ShapeDtypeStruct((r, d), F32),
        compiler_params=_cparams("parallel"),
        name="out_proj",
    )(oa, ob, oc, w, x, gt)


def _ffn_kernel(*refs, n_expert, routed, final_norm):
    refs = list(refs)
    x_ref, g_ref, sc_ref, sh_ref, gt_ref = refs[:5]
    pos = 5
    if routed:
        rt_ref = refs[pos]
        pos += 1
    wg_ref, wu_ref, wd_ref = refs[pos:pos + 3]
    pos += 3
    if final_norm:
        gf_ref = refs[pos]
        pos += 1
    o_ref, h_ref, acc_ref = refs[pos:pos + 3]
    if routed:
        gates_ref = refs[pos + 3]
    e = pl.program_id(1)
    f = pl.program_id(2)
    tm = x_ref.shape[0]

    @pl.when((e == 0) & (f == 0))
    def _():
        h = _rms(x_ref[...]) * g_ref[...]
        h = h * (1.0 + sc_ref[...]) + sh_ref[...]
        h_ref[...] = h.astype(BF16)
        acc_ref[...] = jnp.zeros_like(acc_ref)
        if routed:
            lane = _iota2((tm, 128), 1)
            logits = jnp.where(lane < n_expert, _dot_hi(h, rt_ref[...]), NEG)
            m1 = jnp.max(logits, axis=-1, keepdims=True)
            i1 = jnp.min(jnp.where(logits == m1, lane, 128), axis=-1, keepdims=True)
            rest = jnp.where(lane == i1, NEG, logits)
            m2 = jnp.max(rest, axis=-1, keepdims=True)
            i2 = jnp.min(jnp.where(rest == m2, lane, 128), axis=-1, keepdims=True)
            e2 = jnp.exp(m2 - m1)
            w1 = 1.0 / (1.0 + e2)
            gates_ref[...] = jnp.where(lane == i1, w1, 0.0) + jnp.where(lane == i2, e2 * w1, 0.0)

    hb = h_ref[...]
    a = _dot(hb, wg_ref[...])
    u = _dot(hb, wu_ref[...])
    y = _dot((_silu(a) * u).astype(BF16), wd_ref[...])
    if routed:
        lane = _iota2((tm, 128), 1)
        y = y * jnp.sum(jnp.where(lane == e, gates_ref[...], 0.0), axis=-1, keepdims=True)
    acc_ref[...] += y

    @pl.when((e == n_expert - 1) & (f == pl.num_programs(2) - 1))
    def _():
        out = x_ref[...] + (1.0 + gt_ref[...]) * acc_ref[...]
        if final_norm:
            out = _rms(out) * gf_ref[...]
        o_ref[...] = out


def _ffn(x, g, sc, sh, gt, wg, wu, wd, seq_len, router=None, g_final=None):
    r, d = x.shape
    n_expert, _, d_ff = wg.shape
    tm = ROW_TILE
    tf = d_ff // 2
    routed = router is not None
    final_norm = g_final is not None
    ms = _mod_spec(tm, d, seq_len)
    in_specs = [pl.BlockSpec((tm, d), lambda i, e, f: (i, 0)),
                pl.BlockSpec((1, d), lambda i, e, f: (0, 0)), ms, ms, ms]
    args = [x, g, sc, sh, gt]
    if routed:
        in_specs.append(pl.BlockSpec(router.shape, lambda i, e, f: (0, 0)))
        args.append(router)
    in_specs += [pl.BlockSpec((None, d, tf), lambda i, e, f: (e, 0, f)),
                 pl.BlockSpec((None, d, tf), lambda i, e, f: (e, 0, f)),
                 pl.BlockSpec((None, tf, d), lambda i, e, f: (e, f, 0))]
    args += [wg, wu, wd]
    if final_norm:
        in_specs.append(pl.BlockSpec((1, d), lambda i, e, f: (0, 0)))
        args.append(g_final)
    scratch = [pltpu.VMEM((tm, d), BF16), pltpu.VMEM((tm, d), F32)]
    if routed:
        scratch.append(pltpu.VMEM((tm, 128), F32))
    kern = functools.partial(_ffn_kernel, n_expert=n_expert, routed=routed, final_norm=final_norm)
    return pl.pallas_call(
        kern,
        grid=(r // tm, n_expert, d_ff // tf),
        in_specs=in_specs,
        out_specs=pl.BlockSpec((tm, d), lambda i, e, f: (i, 0)),
        out_shape=jax.ShapeDtypeStruct((r, d), F32),
        scratch_shapes=scratch,
        compiler_params=_cparams("parallel", "arbitrary", "arbitrary"),
        name="moe_ffn" if routed else "dense_ffn",
    )(*args)


def _rot_cols(w):
    half = w.shape[-1] // 2
    return jnp.concatenate([-w[..., half:], w[..., :half]], axis=-1)


def _prep_w_in(w):
    d = w.shape[0]
    sizes = (GDN_QKV, GDN_H * GDN_DV, GDN_H, GDN_H, MLA_QR, MLA_KVR, MLA_ROPE,
             FOX_H * FOX_HD, FOX_H * FOX_HD, FOX_H * FOX_HD, FOX_H)
    pieces, off = [], 0
    for s in sizes:
        pieces.append(w[:, off:off + s])
        off += s
    qkv, z, b, a, cq, ckv, kpe, qc, kc, vc, f = pieces
    used = 2 * MLA_ROPE + 2 * GDN_H + FOX_H
    small = jnp.concatenate([kpe, _rot_cols(kpe), b, a, f, jnp.zeros((d, 128 - used), w.dtype)], axis=1)
    return jnp.concatenate([qkv, z, cq, qc, kc, vc, ckv, small], axis=1).astype(BF16)


def _prep_w_qb(w):
    per = MLA_NOPE + MLA_ROPE
    cols = []
    for h in range(MLA_H):
        nope = w[:, h * per:h * per + MLA_NOPE]
        pe = w[:, h * per + MLA_NOPE:(h + 1) * per]
        cols += [nope, pe, _rot_cols(pe)]
    return jnp.concatenate(cols, axis=1).astype(BF16)


def _lane_row(values, offset):
    row = jnp.zeros((1, 128), F32)
    return row.at[0, offset:offset + values.shape[0]].set(values.astype(F32))


def _rope_table(pos):
    half = MLA_ROPE // 2
    inv = jnp.exp(-math.log(ROPE_BASE) * jnp.arange(half, dtype=F32) / half)
    ang = pos.astype(F32)[:, None] * inv[None, :]
    cos, sin = jnp.cos(ang), jnp.sin(ang)
    return jnp.concatenate([cos, cos, sin, sin], axis=-1)


def kernel(x_prompt, x_sample, cache_mla_latent, cache_mla_rope, cache_fox_k, cache_fox_v, cache_fox_logf, state_gdn, state_gdn_conv, page_table, c_prompt, c_sample, w_ada, b_ada, g_norm1, g_norm2, w_in, gdn_conv_w, gdn_a_log, gdn_dt_bias, gdn_norm_g, mla_q_norm_g, mla_w_qb, mla_kv_norm_g, mla_w_kvb, fox_f_bias, w_out, ffn_w_gate, ffn_w_up, ffn_w_down, moe_router, moe_w_gate, moe_w_up, moe_w_down, g_final):
    n_layer = w_in.shape[0]
    bp, tp, d = x_prompt.shape
    bs, ts, _ = x_sample.shape
    n_pages, page = page_table.shape[1], cache_mla_latent.shape[2]
    past_len = n_pages * page
    assert tp % ROW_TILE == 0 and (bs * ts) % ROW_TILE == 0 and ROW_TILE % ts == 0
    assert tp >= GDN_CONV - 1 and ts >= GDN_CONV - 1 and n_pages % PAGES_PER_STEP == 0

    n_c = bp + bs
    c_all = jnp.concatenate([c_prompt, c_sample, jnp.zeros((-n_c % 8, d), F32)], axis=0)
    mod = _adaln(c_all, w_ada, b_ada)

    lp = jnp.swapaxes(cache_fox_logf, 2, 3)
    lp8 = jnp.concatenate([lp, lp], axis=2)
    pool = cache_fox_k.shape[1]
    fox_kt = jnp.transpose(cache_fox_k, (0, 1, 3, 4, 2)).reshape(n_layer, pool, FOX_H * FOX_HD, page)
    fox_vt = jnp.transpose(cache_fox_v, (0, 1, 3, 4, 2)).reshape(n_layer, pool, FOX_H * FOX_HD, page)
    mla_pe_t = jnp.swapaxes(cache_mla_rope, 2, 3)

    cs_p = _rope_table(jnp.arange(tp))
    cs_s = jnp.tile(_rope_table(past_len + jnp.arange(ts)), (bs, 1))
    head_mask = (jnp.arange(FOX_H)[:, None] == (jnp.arange(FOX_H * FOX_HD) // FOX_HD)[None, :])

    xp = x_prompt.reshape(bp * tp, d)
    xs = x_sample.reshape(bs * ts, d)
    outs = {k: [] for k in ("lat_p", "lat_s", "pe_p", "pe_s", "fk_p", "fk_s", "fv_p", "fv_s",
                            "lf_p", "lf_s", "st_p", "st_s", "cv_p", "cv_s")}

    for l in range(n_layer):
        w_in_l = _prep_w_in(w_in[l])
        w_qb_l = _prep_w_qb(mla_w_qb[l])
        w_kvb = mla_w_kvb[l].reshape(MLA_KVR, MLA_H, MLA_NOPE + MLA_V)
        w_kb_t = jnp.transpose(w_kvb[..., :MLA_NOPE], (1, 2, 0)).astype(BF16)
        w_vb_h = jnp.transpose(w_kvb[..., MLA_NOPE:], (1, 0, 2)).astype(BF16)
        w_vb_all = w_kvb[..., MLA_NOPE:].reshape(MLA_KVR, MLA_H * MLA_V).astype(BF16)
        w_out_l = w_out[l].astype(BF16)
        alog_row = _lane_row(gdn_a_log[l], SM_A)
        dtb_row = _lane_row(gdn_dt_bias[l], SM_A)
        fb_row = _lane_row(fox_f_bias[l], SM_F)
        gn = gdn_norm_g[l].reshape(1, GDN_DV)
        gq = mla_q_norm_g[l].reshape(1, MLA_QR)
        gkv = mla_kv_norm_g[l].reshape(1, MLA_KVR)
        g1 = g_norm1[l].reshape(1, d)
        g2 = g_norm2[l].reshape(1, d)
        cw = gdn_conv_w[l]
        i_ffn = l // 2
        last = l == n_layer - 1

        mods_p = [m.reshape(bp, 1, d) for m in jnp.split(mod[l, :bp], 6, axis=-1)]
        mods_s = [jnp.repeat(m, ts, axis=0) for m in jnp.split(mod[l, bp:bp + bs], 6, axis=-1)]

        def mixer_tail(x, mods, oa, ob, oc, seq_len):
            x = _out_proj(oa, ob, oc, w_out_l, x, mods[2], seq_len)
            gf = g_final.reshape(1, d) if last else None
            if l % 2 == 0:
                return _ffn(x, g2, mods[4], mods[3], mods[5], ffn_w_gate[i_ffn][None].astype(BF16),
                            ffn_w_up[i_ffn][None].astype(BF16), ffn_w_down[i_ffn][None].astype(BF16),
                            seq_len, g_final=gf)
            router = jnp.pad(moe_router[i_ffn], ((0, 0), (0, 128 - moe_router.shape[-1])))
            return _ffn(x, g2, mods[4], mods[3], mods[5], moe_w_gate[i_ffn].astype(BF16),
                        moe_w_up[i_ffn].astype(BF16), moe_w_down[i_ffn].astype(BF16),
                        seq_len, router=router, g_final=gf)

        y = _in_proj(xp, g1, mods_p[1], mods_p[0], w_in_l, tp)
        oa, st = _gdn_prompt(y, cw, alog_row, dtb_row, gn, bp, tp)
        qcat, kcat, lat, kpe = _mla_prep(y, cs_p, gq, gkv, w_qb_l, w_kb_t)
        ob = _flash(qcat, kcat, kcat, bp, tp, shared_kv=True, wp=w_vb_h, d_v=MLA_KVR, d_out=MLA_V)
        fq, fk, fv, fx = _fox_prep(y, fb_row, tp, True)
        g_cum = fx[:, 128 + SM_F:128 + SM_F + FOX_H]
        oc = _flash(fq, fk, fv, bp, tp, shared_kv=False, gq=g_cum.T[:, :, None], gk=g_cum.T[:, None, :],
                    d_v=FOX_HD, d_out=FOX_HD)
        y3 = y.reshape(bp, tp, N_Y)
        outs["lat_p"].append(lat.reshape(bp, tp, MLA_KVR))
        outs["pe_p"].append(kpe.reshape(bp, tp, MLA_ROPE))
        outs["fk_p"].append(y3[:, :, C_KC:C_KC + FOX_H * FOX_HD].reshape(bp, tp, FOX_H, FOX_HD))
        outs["fv_p"].append(y3[:, :, C_VC:C_VC + FOX_H * FOX_HD].reshape(bp, tp, FOX_H, FOX_HD))
        outs["lf_p"].append(fx[:, SM_F:SM_F + FOX_H].reshape(bp, tp, FOX_H))
        outs["st_p"].append(st)
        outs["cv_p"].append(y3[:, tp - (GDN_CONV - 1):, C_QKV:C_QKV + GDN_QKV])
        xp = mixer_tail(xp, mods_p, oa, ob, oc, tp)

        y = _in_proj(xs, g1, mods_s[1], mods_s[0], w_in_l, ts)
        y3 = y.reshape(bs, ts, N_Y)
        oa, st = _gdn_decode(y3, state_gdn_conv[l], state_gdn[l], cw, alog_row, dtb_row, gn)
        qcat, kcat, lat, kpe = _mla_prep(y, cs_s, gq, gkv, w_qb_l, w_kb_t)
        q_rows = jnp.transpose(qcat.reshape(MLA_H, bs, ts, 256), (1, 2, 0, 3)).reshape(bs, ts * MLA_H, 256)
        knew = jnp.pad(kcat.reshape(bs, ts, 256), ((0, 0), (0, 128 - ts), (0, 0)))
        ob = _mla_decode(page_table, q_rows, knew, w_vb_all, cache_mla_latent, mla_pe_t, l, ts)
        (fx,) = _fox_prep(y, fb_row, ts, False)
        g_rows = fx[:, 128 + SM_F:128 + SM_F + FOX_H].reshape(bs, ts * FOX_H, 1)
        qc3 = y3[:, :, C_QC:C_QC + FOX_H * FOX_HD]
        q_bd = jnp.where(head_mask[None, None], qc3[:, :, None, :], 0.0).reshape(bs, ts * FOX_H, FOX_H * FOX_HD)
        kc3 = y3[:, :, C_KC:C_KC + FOX_H * FOX_HD]
        vc3 = y3[:, :, C_VC:C_VC + FOX_H * FOX_HD]
        pad_new = ((0, 0), (0, 128 - ts), (0, 0))
        oc = _fox_decode(page_table, q_bd.astype(BF16), g_rows, jnp.pad(kc3, pad_new).astype(BF16),
                         jnp.pad(vc3, pad_new).astype(BF16), fox_kt, fox_vt, lp8, l, ts)
        outs["lat_s"].append(lat.reshape(bs, ts, MLA_KVR))
        outs["pe_s"].append(kpe.reshape(bs, ts, MLA_ROPE))
        outs["fk_s"].append(kc3.reshape(bs, ts, FOX_H, FOX_HD))
        outs["fv_s"].append(vc3.reshape(bs, ts, FOX_H, FOX_HD))
        outs["lf_s"].append(fx[:, SM_F:SM_F + FOX_H].reshape(bs, ts, FOX_H))
        outs["st_s"].append(st)
        outs["cv_s"].append(y3[:, ts - (GDN_CONV - 1):, C_QKV:C_QKV + GDN_QKV])
        xs = mixer_tail(xs, mods_s, oa.reshape(bs * ts, GDN_H * GDN_DV), ob.reshape(bs * ts, MLA_H * MLA_V),
                        oc.reshape(bs * ts, FOX_H * FOX_HD), ts)

    stack = lambda k: jnp.stack(outs[k], axis=0)
    return (xp.reshape(bp, tp, d), xs.reshape(bs, ts, d),
            stack("lat_p"), stack("lat_s"), stack("pe_p"), stack("pe_s"),
            stack("fk_p"), stack("fk_s"), stack("fv_p"), stack("fv_s"),
            stack("lf_p"), stack("lf_s"), stack("st_p"), stack("st_s"),
            stack("cv_p"), stack("cv_s"))
```
